```python
import math
import jax, jax.numpy as jnp
from jax import lax
import numpy as np

D_MODEL = 1024
BATCH = 4
SEQ = 4096
DEPTH = 1

D_MIX = D_MODEL
M_HEADS = 4
M_DH = D_MODEL // 8
M_WIDTH = M_HEADS * M_DH
CONV_W = 4
CHUNK = 64
A_HEADS = 8
A_DH = D_MODEL // 16
A_WIDTH = A_HEADS * A_DH
IDX_HEADS = 8
IDX_DH = 64
TOPK_MAX = 256
Q_BLOCK = 128
P_HEADS = 8
N_KEYS = 128
N_EXPERTS = N_KEYS * N_KEYS
P_DK = 128
P_TOPK = 16
P_TOK_BLOCK = 128
EPS = 1e-6

_COL_SIZES = (M_WIDTH, M_WIDTH, M_WIDTH, M_WIDTH, M_HEADS, M_HEADS,
              A_WIDTH, A_DH, A_DH, IDX_HEADS * IDX_DH, IDX_DH, IDX_HEADS)
D_IN = 4 * M_WIDTH + 2 * M_HEADS + A_WIDTH + 2 * A_DH + IDX_HEADS * IDX_DH + IDX_DH + IDX_HEADS
F_GATE_OFFSET = 4 * M_WIDTH + M_HEADS

kernel_name = "hymba_mlstm_dsa_peer_block"


def _rmsnorm(x, g):
    xf = x.astype(jnp.float32)
    y = xf * lax.rsqrt(jnp.mean(xf * xf, axis=-1, keepdims=True) + EPS)
    return (y * g.astype(jnp.float32)).astype(x.dtype)


def _headnorm(h, g):
    B, T, H, d = h.shape
    return _rmsnorm(h, g.reshape(H, d)).reshape(B, T, H * d)


def _split_cols(z):
    parts, off = [], 0
    for n in _COL_SIZES:
        parts.append(z[..., off:off + n])
        off += n
    return parts


def _causal_conv(x, w):
    T = x.shape[1]
    xp = jnp.pad(x, ((0, 0), (CONV_W - 1, 0), (0, 0)))
    out = xp[:, 0:T] * w[0]
    for j in range(1, CONV_W):
        out = out + xp[:, j:j + T] * w[j]
    return out


def _mlstm(q, k, v, i_pre, f_pre):
    B, T, H, Dk = q.shape
    Dv = v.shape[-1]
    NC = T // CHUNK

    def to_chunks(a):
        a = a.astype(jnp.float32).reshape((B, NC, CHUNK, H) + a.shape[3:])
        return jnp.moveaxis(a, 3, 1)

    q = to_chunks(q) * (Dk ** -0.5)
    k = to_chunks(k)
    v = to_chunks(v)
    ig = to_chunks(i_pre)
    b = jnp.cumsum(jax.nn.log_sigmoid(to_chunks(f_pre)), -1)
    b_last = b[..., -1]

    causal = jnp.tril(jnp.ones((CHUNK, CHUNK), dtype=bool))
    log_d = jnp.where(causal, b[..., :, None] - b[..., None, :] + ig[..., None, :], -jnp.inf)

    a = b_last[..., None] - b + ig
    m_loc = jnp.max(a, axis=-1)
    w_loc = jnp.exp(a - m_loc[..., None])
    kv = jnp.einsum('bhcsd,bhcse->bhcde', k * w_loc[..., None], v)
    ks = jnp.einsum('bhcs,bhcsd->bhcd', w_loc, k)

    def step(carry, inp):
        C, n, m = carry
        bl, ml, kvc, ksc = inp
        m_new = jnp.maximum(bl + m, ml)
        dec = jnp.exp(bl + m - m_new)
        inj = jnp.exp(ml - m_new)
        C_new = dec[..., None, None] * C + inj[..., None, None] * kvc
        n_new = dec[..., None] * n + inj[..., None] * ksc
        return (C_new, n_new, m_new), (C, n, m)

    init = (jnp.zeros((B, H, Dk, Dv), jnp.float32),
            jnp.zeros((B, H, Dk), jnp.float32),
            jnp.zeros((B, H), jnp.float32))
    xs = (jnp.moveaxis(b_last, 2, 0), jnp.moveaxis(m_loc, 2, 0),
          jnp.moveaxis(kv, 2, 0), jnp.moveaxis(ks, 2, 0))
    _, (C0, n0, m0) = lax.scan(step, init, xs)
    C0 = jnp.moveaxis(C0, 0, 2)
    n0 = jnp.moveaxis(n0, 0, 2)
    m0 = jnp.moveaxis(m0, 0, 2)

    m_inter = b + m0[..., None]
    m_t = jnp.maximum(m_inter, jnp.max(log_d, axis=-1))
    w_inter = jnp.exp(m_inter - m_t)
    s = jnp.einsum('bhctd,bhcsd->bhcts', q, k) * jnp.exp(log_d - m_t[..., None])
    num = (jnp.einsum('bhcts,bhcse->bhcte', s, v)
           + w_inter[..., None] * jnp.einsum('bhctd,bhcde->bhcte', q, C0))
    den = jnp.sum(s, axis=-1) + w_inter * jnp.einsum('bhctd,bhcd->bhct', q, n0)
    h = num / jnp.maximum(jnp.abs(den), jnp.exp(-m_t))[..., None]
    return jnp.moveaxis(h, 1, 3).reshape(B, T, H, Dv)


def _dsa(q, k, v, qi, ki, wi):
    B, T, H, dh = q.shape
    n_sel = min(TOPK_MAX, T // 4)
    NB = T // Q_BLOCK
    idx_scale = (IDX_DH ** -0.5) * (IDX_HEADS ** -0.5)
    att_scale = dh ** -0.5
    key_pos = jnp.arange(T)

    def blocks(a):
        return jnp.moveaxis(a.reshape((B, NB, Q_BLOCK) + a.shape[2:]), 1, 0)

    def block(args):
        qb, qib, wib, tb = args
        rel = jax.nn.relu(jnp.einsum('bqhd,bsd->bqhs', qib, ki))
        score = jnp.einsum('bqhs,bqh->bqs', rel, wib).astype(jnp.float32) * idx_scale
        admissible = key_pos[None, :] <= tb[:, None]
        score = jnp.where(admissible[None], score, -jnp.inf)
        _, idx = lax.top_k(score, n_sel)
        kg = jax.vmap(lambda kk, ii: kk[ii])(k, idx)
        vg = jax.vmap(lambda vv, ii: vv[ii])(v, idx)
        s = jnp.einsum('bqhd,bqkd->bqhk', qb, kg).astype(jnp.float32) * att_scale
        ok = idx <= tb[None, :, None]
        s = jnp.where(ok[:, :, None, :], s, -jnp.inf)
        p = jax.nn.softmax(s, axis=-1).astype(vg.dtype)
        return jnp.einsum('bqhk,bqkd->bqhd', p, vg)

    out = lax.map(block, (blocks(q), blocks(qi), blocks(wi), key_pos.reshape(NB, Q_BLOCK)))
    return jnp.moveaxis(out, 0, 1).reshape(B, T, H, dh)


def _peer(h, w_q, sub_keys, u_tab, v_tab):
    B, T, D = h.shape
    n_tok = B * T
    xt = h.reshape(n_tok, D)
    qry = (xt @ w_q).reshape(n_tok, P_HEADS, 2, P_DK // 2)
    s = jnp.einsum('nhpd,hpkd->nhpk', qry, sub_keys).astype(jnp.float32)
    s1, i1 = lax.top_k(s[:, :, 0], P_TOPK)
    s2, i2 = lax.top_k(s[:, :, 1], P_TOPK)
    cand = (s1[..., :, None] + s2[..., None, :]).reshape(n_tok, P_HEADS, P_TOPK * P_TOPK)
    cidx = (i1[..., :, None] * N_KEYS + i2[..., None, :]).reshape(n_tok, P_HEADS, P_TOPK * P_TOPK)
    top, pos = lax.top_k(cand, P_TOPK)
    eidx = jnp.take_along_axis(cidx, pos, axis=-1).reshape(n_tok, P_HEADS * P_TOPK)
    gate = jax.nn.softmax(top, axis=-1).reshape(n_tok, P_HEADS * P_TOPK).astype(h.dtype)
    nb = n_tok // P_TOK_BLOCK

    def block(args):
        xb, eb, gb = args
        a = jnp.einsum('nd,nkd->nk', xb, u_tab[eb])
        return jnp.einsum('nk,nkd->nd', gb * jax.nn.gelu(a, approximate=False), v_tab[eb])

    out = lax.map(block, (xt.reshape(nb, P_TOK_BLOCK, D),
                          eidx.reshape(nb, P_TOK_BLOCK, P_HEADS * P_TOPK),
                          gate.reshape(nb, P_TOK_BLOCK, P_HEADS * P_TOPK)))
    return out.reshape(B, T, D)


def setup_inputs(seed: int = 0) -> dict:
    key = jax.random.key(seed)
    ks = jax.random.split(key, 20)
    f32 = jnp.float32
    D = D_MODEL
    nrm = lambda k, s: jax.random.normal(k, s, f32)
    x = nrm(ks[0], (BATCH, SEQ, D))
    c = nrm(ks[1], (BATCH, D))
    w_ada = nrm(ks[2], (DEPTH, D, 6 * D)) * (0.5 * D ** -0.5)
    b_ada = nrm(ks[3], (DEPTH, 6 * D)) * 0.02
    norm1_g = 1.0 + 0.02 * nrm(ks[4], (DEPTH, D))
    w_in = nrm(ks[5], (DEPTH, D, D_IN)) * D ** -0.5
    f_bias = jnp.linspace(3.0, 6.0, M_HEADS, dtype=f32)
    b_in = (0.02 * nrm(ks[6], (DEPTH, D_IN))).at[:, F_GATE_OFFSET:F_GATE_OFFSET + M_HEADS].add(f_bias)
    conv_qk = nrm(ks[7], (DEPTH, CONV_W, 2 * M_WIDTH)) * CONV_W ** -0.5
    out_norm_m = 1.0 + 0.02 * nrm(ks[8], (DEPTH, M_WIDTH))
    out_norm_a = 1.0 + 0.02 * nrm(ks[9], (DEPTH, A_WIDTH))
    w_out = nrm(ks[10], (DEPTH, D_MIX, D)) * D_MIX ** -0.5
    norm2_g = 1.0 + 0.02 * nrm(ks[11], (DEPTH, D))
    w_peer_q = nrm(ks[12], (DEPTH, D, P_HEADS * P_DK)) * D ** -0.5
    peer_sub_keys = nrm(ks[13], (DEPTH, P_HEADS, 2, N_KEYS, P_DK // 2)) * (P_DK // 2) ** -0.5
    peer_u = nrm(ks[14], (DEPTH, N_EXPERTS, D)) * D ** -0.5
    peer_v = nrm(ks[15], (DEPTH, N_EXPERTS, D)) * P_HEADS ** -0.5
    final_g = 1.0 + 0.02 * nrm(ks[16], (D,))
    return {"x": x, "c": c, "w_ada": w_ada, "b_ada": b_ada, "norm1_g": norm1_g,
            "w_in": w_in, "b_in": b_in, "conv_qk": conv_qk, "out_norm_m": out_norm_m,
            "out_norm_a": out_norm_a, "w_out": w_out, "norm2_g": norm2_g,
            "w_peer_q": w_peer_q, "peer_sub_keys": peer_sub_keys, "peer_u": peer_u,
            "peer_v": peer_v, "final_g": final_g}


def reference(x, c, w_ada, b_ada, norm1_g, w_in, b_in, conv_qk, out_norm_m, out_norm_a,
              w_out, norm2_g, w_peer_q, peer_sub_keys, peer_u, peer_v, final_g):
    B, T, _ = x.shape
    for l in range(DEPTH):
        ada = jax.nn.silu(c) @ w_ada[l] + b_ada[l]
        sh1, sc1, g1, sh2, sc2, g2 = jnp.split(ada, 6, axis=-1)
        h = _rmsnorm(x, norm1_g[l]) * (1.0 + sc1[:, None]) + sh1[:, None]
        z = h @ w_in[l] + b_in[l]
        mq, mk, mv, mo, mi, mf, aq, ak, av, iq, ik, iw = _split_cols(z)
        qk = jax.nn.silu(_causal_conv(jnp.concatenate([mq, mk], axis=-1), conv_qk[l]))
        mq, mk = qk[..., :M_WIDTH], qk[..., M_WIDTH:]
        hm = _mlstm(mq.reshape(B, T, M_HEADS, M_DH), mk.reshape(B, T, M_HEADS, M_DH),
                    mv.reshape(B, T, M_HEADS, M_DH), mi, mf)
        hm = _headnorm(hm, out_norm_m[l]).astype(x.dtype) * jax.nn.sigmoid(mo)
        ha = _dsa(aq.reshape(B, T, A_HEADS, A_DH), ak, av,
                  iq.reshape(B, T, IDX_HEADS, IDX_DH), ik, iw)
        ha = _headnorm(ha, out_norm_a[l]).astype(x.dtype)
        y = jnp.concatenate([hm, ha], axis=-1) @ w_out[l]
        x = x + g1[:, None] * y
        h2 = _rmsnorm(x, norm2_g[l]) * (1.0 + sc2[:, None]) + sh2[:, None]
        x = x + g2[:, None] * _peer(h2, w_peer_q[l], peer_sub_keys[l], peer_u[l], peer_v[l])
    return _rmsnorm(x, final_g)
```

```python
import functools

import jax
import jax.numpy as jnp
from jax import lax
from jax.experimental import pallas as pl
from jax.experimental.pallas import tpu as pltpu

F32 = jnp.float32
I32 = jnp.int32
MXU_DT = jnp.bfloat16
EPS = 1e-6

M_HEADS = 4
CONV_W = 4
CHUNK = 64
A_HEADS = 8
IDX_HEADS = 8
IDX_DH = 64
TOPK_MAX = 256
Q_BLOCK = 128
P_HEADS = 8
N_KEYS = 128
P_TOPK = 16

LANES = 128
SUBLANES = 8
VMEM_LIMIT = 56 * 1024 * 1024

INT_MIN = -(2 ** 31)
NEG_BIG = -1e30


def _params(sem, vmem=VMEM_LIMIT):
    return pltpu.CompilerParams(dimension_semantics=sem, vmem_limit_bytes=vmem)


def _nt_dot(a, b):
    return lax.dot_general(a, b, (((1,), (1,)), ((), ())), preferred_element_type=F32)


def _dot(a, b):
    return jnp.dot(a, b, preferred_element_type=F32)


def _ada_kernel(c_ref, w_ref, b_ref, o_ref):
    c = c_ref[...]
    sc = c * (1.0 / (1.0 + jnp.exp(-c)))
    o_ref[...] = jnp.dot(sc, w_ref[...], preferred_element_type=F32,
                         precision=lax.Precision.HIGHEST) + b_ref[...]


def _ada(c, w, b):
    B, D = c.shape
    N = w.shape[1]
    return pl.pallas_call(
        _ada_kernel,
        grid=(N // D,),
        in_specs=[pl.BlockSpec((B, D), lambda j: (0, 0)),
                  pl.BlockSpec((D, D), lambda j: (0, j)),
                  pl.BlockSpec((1, D), lambda j: (0, j))],
        out_specs=pl.BlockSpec((B, D), lambda j: (0, j)),
        out_shape=jax.ShapeDtypeStruct((B, N), F32),
        compiler_params=_params(("arbitrary",)),
        name="ada",
    )(c, w, b.reshape(1, N))


def _modulated_norm(x, g, sc, sh):
    ms = jnp.mean(x * x, axis=-1, keepdims=True)
    return (x * lax.rsqrt(ms + EPS) * g) * (1.0 + sc) + sh


def _inproj_kernel(x_ref, g_ref, sc_ref, sh_ref, w_ref, b_ref, wt_ref, bt_ref, z_ref, aux_ref, *, col_chunk):
    h = _modulated_norm(x_ref[0], g_ref[...], sc_ref[0], sh_ref[0])
    hb = h.astype(MXU_DT)
    n_cols = w_ref.shape[1]
    for c0 in range(0, n_cols, col_chunk):
        z_ref[0, :, c0:c0 + col_chunk] = _dot(hb, w_ref[:, c0:c0 + col_chunk]) + b_ref[:, c0:c0 + col_chunk]
    aux_ref[0] = _nt_dot(wt_ref[...], hb) + bt_ref[...]


def _inproj(x, g, sc, sh, w, b, wt, bt, tm):
    B, T, D = x.shape
    NZ = w.shape[1]
    NA = wt.shape[0]
    return pl.pallas_call(
        functools.partial(_inproj_kernel, col_chunk=256),
        grid=(B, T // tm),
        in_specs=[pl.BlockSpec((1, tm, D), lambda b_, i: (b_, i, 0)),
                  pl.BlockSpec((1, D), lambda b_, i: (0, 0)),
                  pl.BlockSpec((1, 1, D), lambda b_, i: (b_, 0, 0)),
                  pl.BlockSpec((1, 1, D), lambda b_, i: (b_, 0, 0)),
                  pl.BlockSpec((D, NZ), lambda b_, i: (0, 0)),
                  pl.BlockSpec((1, NZ), lambda b_, i: (0, 0)),
                  pl.BlockSpec((NA, D), lambda b_, i: (0, 0)),
                  pl.BlockSpec((NA, 1), lambda b_, i: (0, 0))],
        out_specs=[pl.BlockSpec((1, tm, NZ), lambda b_, i: (b_, i, 0)),
                   pl.BlockSpec((1, NA, tm), lambda b_, i: (b_, 0, i))],
        out_shape=[jax.ShapeDtypeStruct((B, T, NZ), F32),
                   jax.ShapeDtypeStruct((B, NA, T), F32)],
        compiler_params=_params(("parallel", "parallel")),
        name="inproj",
    )(x, g, sc, sh, w, b, wt, bt)


Z_MQ, Z_MK, Z_MV, Z_MO, Z_AQ, Z_IQ = 0, 512, 1024, 1536, 2048, 2560
Z_AK, Z_AV, Z_IK, Z_MI, Z_MF, Z_IW, Z_COLS = 3072, 3136, 3200, 3264, 3268, 3272, 3328
AUX_AV, AUX_MI, AUX_MF, AUX_IW, AUX_ROWS = 0, 64, 68, 72, 128


def _log_sigmoid(f):
    return jnp.minimum(f, 0.0) - jnp.log1p(jnp.exp(-jnp.abs(f)))


def _mlstm_kernel(qk_ref, vo_ref, gz_ref, aux_ref, cw_ref, gm_ref, o_ref, hist, c_s, n_s, m_s, *, rows, dh):
    i = pl.program_id(1)
    halo = SUBLANES
    L = CHUNK

    @pl.when(i == 0)
    def _():
        hist[0:halo, :] = jnp.zeros((halo, hist.shape[1]), F32)
        c_s[...] = jnp.zeros(c_s.shape, F32)
        n_s[...] = jnp.zeros(n_s.shape, F32)
        m_s[...] = jnp.zeros(m_s.shape, F32)

    hist[halo:halo + rows, :] = qk_ref[0]
    acc = hist[halo - 3:halo - 3 + rows, :] * cw_ref[0:1, :]
    for j in range(1, CONV_W):
        acc = acc + hist[halo - 3 + j:halo - 3 + j + rows, :] * cw_ref[j:j + 1, :]
    hist[0:halo, :] = hist[rows:rows + halo, :]
    qk = acc * (1.0 / (1.0 + jnp.exp(-acc)))
    width = M_HEADS * dh

    tt = lax.broadcasted_iota(I32, (L, L), 0)
    ss = lax.broadcasted_iota(I32, (L, L), 1)
    causal = ss <= tt

    for c in range(rows // L):
        r0 = c * L
        for h in range(M_HEADS):
            qh = qk[r0:r0 + L, h * dh:(h + 1) * dh] * (dh ** -0.5)
            kh = qk[r0:r0 + L, width + h * dh:width + (h + 1) * dh]
            vh = vo_ref[0, r0:r0 + L, h * dh:(h + 1) * dh]
            oh = vo_ref[0, r0:r0 + L, width + h * dh:width + (h + 1) * dh]
            gcol = Z_MI - Z_IK
            ig_col = gz_ref[0, r0:r0 + L, gcol + h:gcol + h + 1]
            fg_col = _log_sigmoid(gz_ref[0, r0:r0 + L, gcol + M_HEADS + h:gcol + M_HEADS + h + 1])
            ig_row = aux_ref[0, AUX_MI + h:AUX_MI + h + 1, r0:r0 + L]
            fg_row = _log_sigmoid(aux_ref[0, AUX_MF + h:AUX_MF + h + 1, r0:r0 + L])

            b_col = jnp.sum(jnp.where(causal, fg_row, 0.0), axis=1, keepdims=True)
            b_row = jnp.sum(jnp.where(tt <= ss, fg_col, 0.0), axis=0, keepdims=True)
            b_last = jnp.sum(fg_row, axis=1, keepdims=True)
            log_d = jnp.where(causal, b_col - b_row + ig_row, -jnp.inf)
            max_d = jnp.max(log_d, axis=1, keepdims=True)
            m_loc = jnp.max(b_last - b_row + ig_row, axis=1, keepdims=True)
            w_col = jnp.exp(b_last - b_col + ig_col - m_loc)
            kw = kh * w_col
            kv = _dot(kw.T.astype(MXU_DT), vh.astype(MXU_DT))
            ks = jnp.sum(kw, axis=0, keepdims=True)

            c0 = c_s[h]
            n0 = n_s[h:h + 1, :]
            m0 = m_s[h:h + 1, 0:1]
            m_inter = b_col + m0
            m_t = jnp.maximum(m_inter, max_d)
            w_inter = jnp.exp(m_inter - m_t)
            qb = qh.astype(MXU_DT)
            s = _nt_dot(qb, kh.astype(MXU_DT)) * jnp.exp(log_d - m_t)
            num = _dot(s.astype(MXU_DT), vh.astype(MXU_DT)) + w_inter * _dot(qb, c0.astype(MXU_DT))
            den = jnp.sum(s, axis=1, keepdims=True) + w_inter * jnp.sum(qh * n0, axis=1, keepdims=True)
            hh = num / jnp.maximum(jnp.abs(den), jnp.exp(-m_t))

            m_new = jnp.maximum(b_last + m0, m_loc)
            dec = jnp.exp(b_last + m0 - m_new)
            inj = jnp.exp(m_loc - m_new)
            c_s[h] = dec * c0 + inj * kv
            n_s[h:h + 1, :] = dec * n0 + inj * ks
            m_s[h:h + 1, :] = jnp.broadcast_to(m_new, (1, m_s.shape[1]))

            hn = hh * lax.rsqrt(jnp.mean(hh * hh, axis=-1, keepdims=True) + EPS) * gm_ref[:, h * dh:(h + 1) * dh]
            o_ref[0, r0:r0 + L, h * dh:(h + 1) * dh] = hn * (1.0 / (1.0 + jnp.exp(-oh)))


def _mlstm(z, aux, conv_w, gm, rows):
    B, T, _ = z.shape
    width = gm.shape[1]
    dh = width // M_HEADS
    kern = functools.partial(_mlstm_kernel, rows=rows, dh=dh)
    return pl.pallas_call(
        kern,
        grid=(B, T // rows),
        in_specs=[pl.BlockSpec((1, rows, 2 * width), lambda b_, i: (b_, i, 0)),
                  pl.BlockSpec((1, rows, 2 * width), lambda b_, i: (b_, i, 1)),
                  pl.BlockSpec((1, rows, LANES), lambda b_, i: (b_, i, Z_IK // LANES)),
                  pl.BlockSpec((1, AUX_ROWS, rows), lambda b_, i: (b_, 0, i)),
                  pl.BlockSpec((CONV_W, 2 * width), lambda b_, i: (0, 0)),
                  pl.BlockSpec((1, width), lambda b_, i: (0, 0))],
        out_specs=pl.BlockSpec((1, rows, width), lambda b_, i: (b_, i, 0)),
        out_shape=jax.ShapeDtypeStruct((B, T, width), F32),
        scratch_shapes=[pltpu.VMEM((rows + SUBLANES, 2 * width), F32),
                        pltpu.VMEM((M_HEADS, dh, dh), F32),
                        pltpu.VMEM((SUBLANES, dh), F32),
                        pltpu.VMEM((SUBLANES, LANES), F32)],
        compiler_params=_params(("parallel", "arbitrary")),
        name="mlstm",
    )(z, z, z, aux, conv_w, gm)


def _dsa_kernel(aq_ref, iq_ref, kv_ref, ik_ref, aux_ref, ga_ref, o_ref, key_s, thr_s, acc_s, ml_s, *, n_sel, dh):
    qb = pl.program_id(1)
    nkb = qb + 1
    QB = Q_BLOCK
    q0 = pl.multiple_of(qb * QB, QB)
    idx_scale = (IDX_DH ** -0.5) * (IDX_HEADS ** -0.5)
    row = lax.broadcasted_iota(I32, (QB, QB), 0)
    col = lax.broadcasted_iota(I32, (QB, QB), 1)

    iqb = iq_ref[0].astype(MXU_DT)
    w_rows = aux_ref[0, AUX_IW:AUX_IW + IDX_HEADS, pl.ds(q0, QB)] * idx_scale

    def score_body(j, carry):
        k0 = pl.multiple_of(j * QB, QB)
        ki = ik_ref[0, pl.ds(k0, QB), 0:IDX_DH].astype(MXU_DT)
        sc = jnp.zeros((QB, QB), F32)
        for h in range(IDX_HEADS):
            r = _nt_dot(ki, iqb[:, h * IDX_DH:(h + 1) * IDX_DH])
            sc = sc + jnp.maximum(r, 0.0) * w_rows[h:h + 1, :]
        bits = lax.bitcast_convert_type(sc, I32)
        key = jnp.where(bits < 0, bits ^ jnp.int32(0x7FFFFFFF), bits)
        key = jnp.where(k0 + row <= q0 + col, key, jnp.int32(INT_MIN))
        key_s[pl.ds(k0, QB), :] = key
        return carry

    lax.fori_loop(0, nkb, score_body, 0)

    def count(cand, strict):
        def body(j, c8):
            k = key_s[pl.ds(pl.multiple_of(j * QB, QB), QB), :]
            hit = (k > cand) if strict else (k >= cand)
            return c8 + jnp.sum(jnp.where(hit, 1, 0).astype(I32).reshape(QB // SUBLANES, SUBLANES, QB), axis=0)
        c8 = lax.fori_loop(0, nkb, body, jnp.zeros((SUBLANES, QB), I32))
        return jnp.sum(c8, axis=0, keepdims=True)

    thr_s[0:1, :] = jnp.full((1, QB), INT_MIN + 1, I32)
    thr_s[1:2, :] = jnp.full((1, QB), n_sel, I32)

    @pl.when(qb >= n_sel // QB)
    def _():
        c_pos = count(jnp.zeros((1, QB), I32), False)
        prefix = jnp.where(c_pos >= n_sel, 0, jnp.int32(INT_MIN)).astype(I32)

        def bit_body(it, prefix):
            cand = prefix | jnp.left_shift(jnp.int32(1), 30 - it)
            c = count(cand, False)
            return jnp.where(c >= n_sel, cand, prefix)

        prefix = lax.fori_loop(0, 31, bit_body, prefix)
        thr_s[0:1, :] = prefix
        thr_s[1:2, :] = n_sel - count(prefix, True)

    thr = thr_s[0:1, :]
    need = thr_s[1:2, :].astype(F32)

    aqb = (aq_ref[0] * (dh ** -0.5)).astype(MXU_DT)
    acc_s[...] = jnp.zeros(acc_s.shape, F32)
    ml_s[0:A_HEADS, :] = jnp.full((A_HEADS, QB), NEG_BIG, F32)
    ml_s[A_HEADS:2 * A_HEADS, :] = jnp.zeros((A_HEADS, QB), F32)
    tri = jnp.where(col < row, 1.0, 0.0).astype(MXU_DT)

    def att_body(j, rc):
        k0 = pl.multiple_of(j * QB, QB)
        kb = kv_ref[0, pl.ds(k0, QB), 0:dh].astype(MXU_DT)
        vt = aux_ref[0, AUX_AV:AUX_AV + dh, pl.ds(k0, QB)].astype(MXU_DT)
        key = key_s[pl.ds(k0, QB), :]
        tie = key == thr
        tie_f = jnp.where(tie, 1.0, 0.0)
        before = _dot(tri, tie_f.astype(MXU_DT)) + rc
        sel = (key > thr) | (tie & (before < need))
        for h in range(A_HEADS):
            st = _nt_dot(kb, aqb[:, h * dh:(h + 1) * dh])
            st = jnp.where(sel, st, NEG_BIG)
            m_old = ml_s[h:h + 1, :]
            m_new = jnp.maximum(m_old, jnp.max(st, axis=0, keepdims=True))
            p = jnp.where(sel, jnp.exp(st - m_new), 0.0)
            alpha = jnp.exp(m_old - m_new)
            ml_s[h:h + 1, :] = m_new
            ml_s[A_HEADS + h:A_HEADS + h + 1, :] = alpha * ml_s[A_HEADS + h:A_HEADS + h + 1, :] + jnp.sum(p, axis=0, keepdims=True)
            acc_s[:, h * QB:(h + 1) * QB] = alpha * acc_s[:, h * QB:(h + 1) * QB] + _dot(vt, p.astype(MXU_DT))
        return rc + jnp.sum(tie_f, axis=0, keepdims=True)

    lax.fori_loop(0, nkb, att_body, jnp.zeros((1, QB), F32))

    for h in range(A_HEADS):
        o = acc_s[:, h * QB:(h + 1) * QB] / ml_s[A_HEADS + h:A_HEADS + h + 1, :]
        o = o * lax.rsqrt(jnp.mean(o * o, axis=0, keepdims=True) + EPS)
        o_ref[0, :, h * dh:(h + 1) * dh] = o.T * ga_ref[:, h * dh:(h + 1) * dh]


def _dsa(z, aux, ga, n_sel):
    B, T, _ = z.shape
    width = ga.shape[1]
    dh = width // A_HEADS
    assert n_sel % Q_BLOCK == 0 and T % Q_BLOCK == 0 and dh == Z_AV - Z_AK
    kern = functools.partial(_dsa_kernel, n_sel=n_sel, dh=dh)
    return pl.pallas_call(
        kern,
        grid=(B, T // Q_BLOCK),
        in_specs=[pl.BlockSpec((1, Q_BLOCK, width), lambda b_, i: (b_, i, Z_AQ // width)),
                  pl.BlockSpec((1, Q_BLOCK, width), lambda b_, i: (b_, i, Z_IQ // width)),
                  pl.BlockSpec((1, T, LANES), lambda b_, i: (b_, 0, Z_AK // LANES)),
                  pl.BlockSpec((1, T, LANES), lambda b_, i: (b_, 0, Z_IK // LANES)),
                  pl.BlockSpec((1, AUX_ROWS, T), lambda b_, i: (b_, 0, 0)),
                  pl.BlockSpec((1, width), lambda b_, i: (0, 0))],
        out_specs=pl.BlockSpec((1, Q_BLOCK, width), lambda b_, i: (b_, i, 0)),
        out_shape=jax.ShapeDtypeStruct((B, T, width), F32),
        scratch_shapes=[pltpu.VMEM((T, Q_BLOCK), I32),
                        pltpu.VMEM((SUBLANES, Q_BLOCK), I32),
                        pltpu.VMEM((dh, A_HEADS * Q_BLOCK), F32),
                        pltpu.VMEM((2 * A_HEADS, Q_BLOCK), F32)],
        compiler_params=_params(("parallel", "arbitrary")),
        name="dsa",
    )(z, z, z, z, aux, ga)


def _post_kernel(hm_ref, ha_ref, x_ref, wo_ref, g1_ref, g_ref, sc_ref, sh_ref, wq_ref, sk_ref,
                 x1_ref, h2t_ref, st_ref):
    wm = hm_ref.shape[2]
    y = _dot(hm_ref[0].astype(MXU_DT), wo_ref[0:wm, :]) + _dot(ha_ref[0].astype(MXU_DT), wo_ref[wm:, :])
    x1 = x_ref[0] + g1_ref[0] * y
    x1_ref[0] = x1
    h2 = _modulated_norm(x1, g_ref[...], sc_ref[0], sh_ref[0])
    h2t_ref[...] = h2.T.astype(h2t_ref.dtype)
    qry = _dot(h2.astype(MXU_DT), wq_ref[...]).astype(MXU_DT)
    dk = sk_ref.shape[2]
    for hp in range(sk_ref.shape[0]):
        st_ref[hp] = _nt_dot(sk_ref[hp], qry[:, hp * dk:(hp + 1) * dk])


def _post(hm, ha, x, wo, g1, g, sc, sh, wq, sk, tm):
    B, T, D = x.shape
    nt = T // tm
    wm = hm.shape[2]
    NS, NK, DK = sk.shape
    return pl.pallas_call(
        _post_kernel,
        grid=(B, nt),
        in_specs=[pl.BlockSpec((1, tm, wm), lambda b_, i: (b_, i, 0)),
                  pl.BlockSpec((1, tm, ha.shape[2]), lambda b_, i: (b_, i, 0)),
                  pl.BlockSpec((1, tm, D), lambda b_, i: (b_, i, 0)),
                  pl.BlockSpec(wo.shape, lambda b_, i: (0, 0)),
                  pl.BlockSpec((1, 1, D), lambda b_, i: (b_, 0, 0)),
                  pl.BlockSpec((1, D), lambda b_, i: (0, 0)),
                  pl.BlockSpec((1, 1, D), lambda b_, i: (b_, 0, 0)),
                  pl.BlockSpec((1, 1, D), lambda b_, i: (b_, 0, 0)),
                  pl.BlockSpec(wq.shape, lambda b_, i: (0, 0)),
                  pl.BlockSpec(sk.shape, lambda b_, i: (0, 0, 0))],
        out_specs=[pl.BlockSpec((1, tm, D), lambda b_, i: (b_, i, 0)),
                   pl.BlockSpec((D, tm), lambda b_, i: (0, b_ * nt + i)),
                   pl.BlockSpec((NS, NK, tm), lambda b_, i: (0, 0, b_ * nt + i))],
        out_shape=[jax.ShapeDtypeStruct((B, T, D), F32),
                   jax.ShapeDtypeStruct((D, B * T), MXU_DT),
                   jax.ShapeDtypeStruct((NS, NK, B * T), F32)],
        compiler_params=_params(("parallel", "parallel")),
        name="post",
    )(hm, ha, x, wo, g1, g, sc, sh, wq, sk)


def _top_sorted(cur, n):
    R = cur.shape[0]
    rows = lax.broadcasted_iota(I32, cur.shape, 0)
    vals = []
    for k in range(n):
        mx = jnp.max(cur, axis=0, keepdims=True)
        vals.append(mx)
        if k + 1 < n:
            first = jnp.min(jnp.where(cur == mx, rows, R), axis=0, keepdims=True)
            cur = jnp.where(rows == first, -jnp.inf, cur)
    return vals


def _select_kernel(st_ref, d1_ref, e1_ref, e2_ref, cand_s):
    n_top = P_TOPK + 1
    pairs = [(a, b) for a in range(n_top) for b in range(n_top) if (a + 1) * (b + 1) <= n_top]
    for h in range(P_HEADS):
        s1 = st_ref[2 * h]
        s2 = st_ref[2 * h + 1]
        v1 = _top_sorted(s1, n_top)
        v2 = _top_sorted(s2, n_top)
        cand_s[...] = jnp.full(cand_s.shape, -jnp.inf, F32)
        for r, (a, b) in enumerate(pairs):
            cand_s[r:r + 1, :] = v1[a] + v2[b]
        top = _top_sorted(cand_s[...], n_top)
        thr = 0.5 * (top[P_TOPK - 1] + top[P_TOPK])
        zsum = jnp.zeros_like(thr)
        for k in range(P_TOPK):
            zsum = zsum + jnp.exp(top[k] - top[0])
        d1_ref[h] = thr - s1
        e1_ref[h] = jnp.exp(s1 - v1[0]) / zsum
        e2_ref[h] = jnp.exp(s2 - v2[0])


def _select(st, tl):
    NS, NK, N = st.shape
    n_top = P_TOPK + 1
    n_pairs = sum(1 for a in range(n_top) for b in range(n_top) if (a + 1) * (b + 1) <= n_top)
    cand_rows = -(-n_pairs // SUBLANES) * SUBLANES
    out = jax.ShapeDtypeStruct((P_HEADS, NK, N), F32)
    spec = pl.BlockSpec((P_HEADS, NK, tl), lambda i: (0, 0, i))
    return pl.pallas_call(
        _select_kernel,
        grid=(N // tl,),
        in_specs=[pl.BlockSpec((NS, NK, tl), lambda i: (0, 0, i))],
        out_specs=[spec, spec, spec],
        out_shape=[out, out, out],
        scratch_shapes=[pltpu.VMEM((cand_rows, tl), F32)],
        compiler_params=_params(("parallel",)),
        name="select",
    )(st)


def _gelu(a):
    return 0.5 * a * (1.0 + lax.erf(a * (2.0 ** -0.5)))


def _expert_kernel(h2t_ref, u_ref, vt_ref, st_ref, d1_ref, e1_ref, e2_ref, o_ref, acc_s, p_s, *, lane_tile):
    e = pl.program_id(1)
    te = u_ref.shape[0]
    tn = h2t_ref.shape[1]
    rows_per_step = te // N_KEYS

    @pl.when(e == 0)
    def _():
        acc_s[...] = jnp.zeros(acc_s.shape, F32)

    a_t = _dot(u_ref[...], h2t_ref[...])
    for ii in range(rows_per_step):
        i = e * rows_per_step + ii
        for l0 in range(0, tn, lane_tile):
            g = jnp.zeros((N_KEYS, lane_tile), F32)
            for h in range(P_HEADS):
                d1 = d1_ref[h, pl.ds(i, 1), l0:l0 + lane_tile]
                e1 = e1_ref[h, pl.ds(i, 1), l0:l0 + lane_tile]
                s2 = st_ref[2 * h + 1, :, l0:l0 + lane_tile]
                e2 = e2_ref[h, :, l0:l0 + lane_tile]
                g = g + jnp.where(s2 > d1, e2 * e1, 0.0)
            a_blk = a_t[ii * N_KEYS:(ii + 1) * N_KEYS, l0:l0 + lane_tile]
            p_s[ii * N_KEYS:(ii + 1) * N_KEYS, l0:l0 + lane_tile] = (g * _gelu(a_blk)).astype(p_s.dtype)
    acc_s[...] += _dot(vt_ref[...], p_s[...])

    @pl.when(e == pl.num_programs(1) - 1)
    def _():
        o_ref[...] = acc_s[...]


def _expert(h2t, u, vt, st, d1, e1, e2, tn, te):
    D, N = h2t.shape
    NE = u.shape[0]
    NS, NK, _ = st.shape
    kern = functools.partial(_expert_kernel, lane_tile=min(tn, 2 * LANES))
    hspec = pl.BlockSpec((P_HEADS, NK, tn), lambda t, e: (0, 0, t))
    return pl.pallas_call(
        kern,
        grid=(N // tn, NE // te),
        in_specs=[pl.BlockSpec((D, tn), lambda t, e: (0, t)),
                  pl.BlockSpec((te, D), lambda t, e: (e, 0)),
                  pl.BlockSpec((D, te), lambda t, e: (0, e)),
                  pl.BlockSpec((NS, NK, tn), lambda t, e: (0, 0, t)),
                  hspec, hspec, hspec],
        out_specs=pl.BlockSpec((D, tn), lambda t, e: (0, t)),
        out_shape=jax.ShapeDtypeStruct((D, N), F32),
        scratch_shapes=[pltpu.VMEM((D, tn), F32),
                        pltpu.VMEM((te, tn), MXU_DT)],
        compiler_params=_params(("parallel", "arbitrary")),
        name="expert",
    )(h2t, u, vt, st, d1, e1, e2)


def _final_kernel(x1_ref, pt_ref, g2_ref, gf_ref, o_ref, *, normalize):
    x2 = x1_ref[0] + g2_ref[0] * pt_ref[...].T
    if normalize:
        ms = jnp.mean(x2 * x2, axis=-1, keepdims=True)
        x2 = x2 * lax.rsqrt(ms + EPS) * gf_ref[...]
    o_ref[0] = x2


def _final(x1, pt, g2, gf, tm, normalize):
    B, T, D = x1.shape
    nt = T // tm
    return pl.pallas_call(
        functools.partial(_final_kernel, normalize=normalize),
        grid=(B, nt),
        in_specs=[pl.BlockSpec((1, tm, D), lambda b_, i: (b_, i, 0)),
                  pl.BlockSpec((D, tm), lambda b_, i: (0, b_ * nt + i)),
                  pl.BlockSpec((1, 1, D), lambda b_, i: (b_, 0, 0)),
                  pl.BlockSpec((1, D), lambda b_, i: (0, 0))],
        out_specs=pl.BlockSpec((1, tm, D), lambda b_, i: (b_, i, 0)),
        out_shape=jax.ShapeDtypeStruct((B, T, D), F32),
        compiler_params=_params(("parallel", "parallel")),
        name="final",
    )(x1, pt, g2, gf)


def _pack_in_weights(w_in, b_in):
    D = w_in.shape[0]
    o_mi, o_aq, o_ak, o_av, o_iq, o_ik, o_iw, o_end = 2048, 2056, 2568, 2632, 2696, 3208, 3272, 3280
    order = [(0, o_mi), (o_aq, o_ak), (o_iq, o_ik), (o_ak, o_av), (o_av, o_iq), (o_ik, o_iw),
             (o_mi, o_aq), (o_iw, o_end)]
    n_used = sum(hi - lo for lo, hi in order)
    w = jnp.concatenate([w_in[:, lo:hi] for lo, hi in order] + [jnp.zeros((D, Z_COLS - n_used), F32)], axis=1)
    b = jnp.concatenate([b_in[lo:hi] for lo, hi in order] + [jnp.zeros((Z_COLS - n_used,), F32)])
    aux_order = [(o_av, o_iq), (o_mi, o_aq), (o_iw, o_end)]
    n_aux = sum(hi - lo for lo, hi in aux_order)
    wt = jnp.concatenate([w_in[:, lo:hi] for lo, hi in aux_order] + [jnp.zeros((D, AUX_ROWS - n_aux), F32)], axis=1).T
    bt = jnp.concatenate([b_in[lo:hi] for lo, hi in aux_order] + [jnp.zeros((AUX_ROWS - n_aux,), F32)])
    return w.astype(MXU_DT), b.reshape(1, Z_COLS), wt.astype(MXU_DT), bt.reshape(AUX_ROWS, 1)


def _layer(x, ada, norm1_g, w_in, b_in, conv_qk, out_norm_m, out_norm_a, w_out, norm2_g,
           w_peer_q, peer_sub_keys, peer_u, peer_v, *, tm, mrows, tl, tn, te):
    B, T, D = x.shape
    sh1, sc1, g1, sh2, sc2, g2 = [a.reshape(B, 1, D) for a in jnp.split(ada, 6, axis=-1)]
    w, b, wt, bt = _pack_in_weights(w_in, b_in)
    z, aux = _inproj(x, norm1_g.reshape(1, D), sc1, sh1, w, b, wt, bt, tm)
    hm = _mlstm(z, aux, conv_qk, out_norm_m.reshape(1, -1), mrows)
    ha = _dsa(z, aux, out_norm_a.reshape(1, -1), min(TOPK_MAX, T // 4))
    sk = peer_sub_keys.reshape(2 * P_HEADS, N_KEYS, -1).astype(MXU_DT)
    x1, h2t, st = _post(hm, ha, x, w_out.astype(MXU_DT), g1, norm2_g.reshape(1, D), sc2, sh2,
                        w_peer_q.astype(MXU_DT), sk, tm)
    d1, e1, e2 = _select(st, tl)
    pt = _expert(h2t, peer_u.astype(MXU_DT), peer_v.T.astype(MXU_DT), st, d1, e1, e2, tn, te)
    return x1, pt, g2


def kernel(x, c, w_ada, b_ada, norm1_g, w_in, b_in, conv_qk, out_norm_m, out_norm_a, w_out, norm2_g,
           w_peer_q, peer_sub_keys, peer_u, peer_v, final_g):
    B, T, D = x.shape
    depth = w_ada.shape[0]
    tm = min(512, T)
    for l in range(depth):
        ada = _ada(c, w_ada[l], b_ada[l])
        x1, pt, g2 = _layer(x, ada, norm1_g[l], w_in[l], b_in[l], conv_qk[l], out_norm_m[l], out_norm_a[l],
                            w_out[l], norm2_g[l], w_peer_q[l], peer_sub_keys[l], peer_u[l], peer_v[l],
                            tm=tm, mrows=min(256, T), tl=256, tn=512, te=512)
        x = _final(x1, pt, g2, final_g.reshape(1, D), tm, normalize=(l + 1 == depth))
    return x
```

```python
import functools

import jax
import jax.numpy as jnp
from jax import lax
from jax.experimental import pallas as pl
from jax.experimental.pallas import tpu as pltpu

F32 = jnp.float32
I32 = jnp.int32
MXU_DT = jnp.bfloat16
EPS = 1e-6

M_HEADS = 4
CONV_W = 4
CHUNK = 64
A_HEADS = 8
IDX_HEADS = 8
IDX_DH = 64
TOPK_MAX = 256
Q_BLOCK = 128
P_HEADS = 8
N_KEYS = 128
P_TOPK = 16

LANES = 128
SUBLANES = 8
VMEM_LIMIT = 56 * 1024 * 1024

INT_MIN = -(2 ** 31)
NEG_BIG = -1e30


def _params(sem, vmem=VMEM_LIMIT):
    return pltpu.CompilerParams(dimension_semantics=sem, vmem_limit_bytes=vmem)


def _nt_dot(a, b):
    return lax.dot_general(a, b, (((1,), (1,)), ((), ())), preferred_element_type=F32)


def _dot(a, b):
    return jnp.dot(a, b, preferred_element_type=F32)


def _ada_kernel(c_ref, w_ref, b_ref, o_ref):
    c = c_ref[...]
    sc = c * (1.0 / (1.0 + jnp.exp(-c)))
    o_ref[...] = jnp.dot(sc, w_ref[...], preferred_element_type=F32,
                         precision=lax.Precision.HIGHEST) + b_ref[...]


def _ada(c, w, b):
    B, D = c.shape
    N = w.shape[1]
    return pl.pallas_call(
        _ada_kernel,
        grid=(N // D,),
        in_specs=[pl.BlockSpec((B, D), lambda j: (0, 0)),
                  pl.BlockSpec((D, D), lambda j: (0, j)),
                  pl.BlockSpec((1, D), lambda j: (0, j))],
        out_specs=pl.BlockSpec((B, D), lambda j: (0, j)),
        out_shape=jax.ShapeDtypeStruct((B, N), F32),
        compiler_params=_params(("arbitrary",)),
        name="ada",
    )(c, w, b.reshape(1, N))


def _modulated_norm(x, g, sc, sh):
    ms = jnp.mean(x * x, axis=-1, keepdims=True)
    return (x * lax.rsqrt(ms + EPS) * g) * (1.0 + sc) + sh


def _inproj_kernel(x_ref, g_ref, sc_ref, sh_ref, w_ref, b_ref, wt_ref, bt_ref, z_ref, aux_ref, *, col_chunk):
    h = _modulated_norm(x_ref[0], g_ref[...], sc_ref[0], sh_ref[0])
    hb = h.astype(MXU_DT)
    n_cols = w_ref.shape[1]
    for c0 in range(0, n_cols, col_chunk):
        z_ref[0, :, c0:c0 + col_chunk] = _dot(hb, w_ref[:, c0:c0 + col_chunk]) + b_ref[:, c0:c0 + col_chunk]
    aux_ref[0] = _nt_dot(wt_ref[...], hb) + bt_ref[...]


def _inproj(x, g, sc, sh, w, b, wt, bt, tm):
    B, T, D = x.shape
    NZ = w.shape[1]
    NA = wt.shape[0]
    return pl.pallas_call(
        functools.partial(_inproj_kernel, col_chunk=256),
        grid=(B, T // tm),
        in_specs=[pl.BlockSpec((1, tm, D), lambda b_, i: (b_, i, 0)),
                  pl.BlockSpec((1, D), lambda b_, i: (0, 0)),
                  pl.BlockSpec((1, 1, D), lambda b_, i: (b_, 0, 0)),
                  pl.BlockSpec((1, 1, D), lambda b_, i: (b_, 0, 0)),
                  pl.BlockSpec((D, NZ), lambda b_, i: (0, 0)),
                  pl.BlockSpec((1, NZ), lambda b_, i: (0, 0)),
                  pl.BlockSpec((NA, D), lambda b_, i: (0, 0)),
                  pl.BlockSpec((NA, 1), lambda b_, i: (0, 0))],
        out_specs=[pl.BlockSpec((1, tm, NZ), lambda b_, i: (b_, i, 0)),
                   pl.BlockSpec((1, NA, tm), lambda b_, i: (b_, 0, i))],
        out_shape=[jax.ShapeDtypeStruct((B, T, NZ), F32),
                   jax.ShapeDtypeStruct((B, NA, T), F32)],
        compiler_params=_params(("parallel", "parallel")),
        name="inproj",
    )(x, g, sc, sh, w, b, wt, bt)


Z_MQ, Z_MK, Z_MV, Z_MO, Z_AQ, Z_IQ = 0, 512, 1024, 1536, 2048, 2560
Z_AK, Z_AV, Z_IK, Z_MI, Z_MF, Z_IW, Z_COLS = 3072, 3136, 3200, 3264, 3268, 3272, 3328
AUX_AV, AUX_MI, AUX_MF, AUX_IW, AUX_ROWS = 0, 64, 68, 72, 128


def _log_sigmoid(f):
    return jnp.minimum(f, 0.0) - jnp.log1p(jnp.exp(-jnp.abs(f)))


def _mlstm_kernel(qk_ref, vo_ref, gz_ref, aux_ref, cw_ref, gm_ref, o_ref, hist, c_s, n_s, m_s, *, rows, dh):
    i = pl.program_id(1)
    halo = SUBLANES
    L = CHUNK

    @pl.when(i == 0)
    def _():
        hist[0:halo, :] = jnp.zeros((halo, hist.shape[1]), F32)
        c_s[...] = jnp.zeros(c_s.shape, F32)
        n_s[...] = jnp.zeros(n_s.shape, F32)
        m_s[...] = jnp.zeros(m_s.shape, F32)

    hist[halo:halo + rows, :] = qk_ref[0]
    acc = hist[halo - 3:halo - 3 + rows, :] * cw_ref[0:1, :]
    for j in range(1, CONV_W):
        acc = acc + hist[halo - 3 + j:halo - 3 + j + rows, :] * cw_ref[j:j + 1, :]
    hist[0:halo, :] = hist[rows:rows + halo, :]
    qk = acc * (1.0 / (1.0 + jnp.exp(-acc)))
    width = M_HEADS * dh

    tt = lax.broadcasted_iota(I32, (L, L), 0)
    ss = lax.broadcasted_iota(I32, (L, L), 1)
    causal = ss <= tt

    for c in range(rows // L):
        r0 = c * L
        for h in range(M_HEADS):
            qh = qk[r0:r0 + L, h * dh:(h + 1) * dh] * (dh ** -0.5)
            kh = qk[r0:r0 + L, width + h * dh:width + (h + 1) * dh]
            vh = vo_ref[0, r0:r0 + L, h * dh:(h + 1) * dh]
            oh = vo_ref[0, r0:r0 + L, width + h * dh:width + (h + 1) * dh]
            gcol = Z_MI - Z_IK
            ig_col = gz_ref[0, r0:r0 + L, gcol + h:gcol + h + 1]
            fg_col = _log_sigmoid(gz_ref[0, r0:r0 + L, gcol + M_HEADS + h:gcol + M_HEADS + h + 1])
            ig_row = aux_ref[0, AUX_MI + h:AUX_MI + h + 1, r0:r0 + L]
            fg_row = _log_sigmoid(aux_ref[0, AUX_MF + h:AUX_MF + h + 1, r0:r0 + L])

            b_col = jnp.sum(jnp.where(causal, fg_row, 0.0), axis=1, keepdims=True)
            b_row = jnp.sum(jnp.where(tt <= ss, fg_col, 0.0), axis=0, keepdims=True)
            b_last = jnp.sum(fg_row, axis=1, keepdims=True)
            log_d = jnp.where(causal, b_col - b_row + ig_row, -jnp.inf)
            max_d = jnp.max(log_d, axis=1, keepdims=True)
            m_loc = jnp.max(b_last - b_row + ig_row, axis=1, keepdims=True)
            w_col = jnp.exp(b_last - b_col + ig_col - m_loc)
            kw = kh * w_col
            kv = _dot(kw.T.astype(MXU_DT), vh.astype(MXU_DT))
            ks = jnp.sum(kw, axis=0, keepdims=True)

            c0 = c_s[h]
            n0 = n_s[h:h + 1, :]
            m0 = m_s[h:h + 1, 0:1]
            m_inter = b_col + m0
            m_t = jnp.maximum(m_inter, max_d)
            w_inter = jnp.exp(m_inter - m_t)
            qb = qh.astype(MXU_DT)
            s = _nt_dot(qb, kh.astype(MXU_DT)) * jnp.exp(log_d - m_t)
            num = _dot(s.astype(MXU_DT), vh.astype(MXU_DT)) + w_inter * _dot(qb, c0.astype(MXU_DT))
            den = jnp.sum(s, axis=1, keepdims=True) + w_inter * jnp.sum(qh * n0, axis=1, keepdims=True)
            hh = num / jnp.maximum(jnp.abs(den), jnp.exp(-m_t))

            m_new = jnp.maximum(b_last + m0, m_loc)
            dec = jnp.exp(b_last + m0 - m_new)
            inj = jnp.exp(m_loc - m_new)
            c_s[h] = dec * c0 + inj * kv
            n_s[h:h + 1, :] = dec * n0 + inj * ks
            m_s[h:h + 1, :] = jnp.broadcast_to(m_new, (1, m_s.shape[1]))

            hn = hh * lax.rsqrt(jnp.mean(hh * hh, axis=-1, keepdims=True) + EPS) * gm_ref[:, h * dh:(h + 1) * dh]
            o_ref[0, r0:r0 + L, h * dh:(h + 1) * dh] = hn * (1.0 / (1.0 + jnp.exp(-oh)))


def _mlstm(z, aux, conv_w, gm, rows):
    B, T, _ = z.shape
    width = gm.shape[1]
    dh = width // M_HEADS
    kern = functools.partial(_mlstm_kernel, rows=rows, dh=dh)
    return pl.pallas_call(
        kern,
        grid=(B, T // rows),
        in_specs=[pl.BlockSpec((1, rows, 2 * width), lambda b_, i: (b_, i, 0)),
                  pl.BlockSpec((1, rows, 2 * width), lambda b_, i: (b_, i, 1)),
                  pl.BlockSpec((1, rows, LANES), lambda b_, i: (b_, i, Z_IK // LANES)),
                  pl.BlockSpec((1, AUX_ROWS, rows), lambda b_, i: (b_, 0, i)),
                  pl.BlockSpec((CONV_W, 2 * width), lambda b_, i: (0, 0)),
                  pl.BlockSpec((1, width), lambda b_, i: (0, 0))],
        out_specs=pl.BlockSpec((1, rows, width), lambda b_, i: (b_, i, 0)),
        out_shape=jax.ShapeDtypeStruct((B, T, width), F32),
        scratch_shapes=[pltpu.VMEM((rows + SUBLANES, 2 * width), F32),
                        pltpu.VMEM((M_HEADS, dh, dh), F32),
                        pltpu.VMEM((SUBLANES, dh), F32),
                        pltpu.VMEM((SUBLANES, LANES), F32)],
        compiler_params=_params(("parallel", "arbitrary")),
        name="mlstm",
    )(z, z, z, aux, conv_w, gm)


def _dsa_kernel(aq_ref, iq_ref, kv_ref, ik_ref, aux_ref, ga_ref, o_ref, key_s, thr_s, acc_s, *, n_sel, dh):
    qb = pl.program_id(1)
    QB = Q_BLOCK
    KB = 2 * QB
    nkb = (qb + 2) // 2
    q0 = pl.multiple_of(qb * QB, QB)
    idx_scale = (IDX_DH ** -0.5) * (IDX_HEADS ** -0.5)
    row = lax.broadcasted_iota(I32, (KB, QB), 0)
    col = lax.broadcasted_iota(I32, (KB, QB), 1)

    iqb = iq_ref[0].astype(MXU_DT)
    w_rows = aux_ref[0, AUX_IW:AUX_IW + IDX_HEADS, pl.ds(q0, QB)] * idx_scale

    def score_body(j, carry):
        k0 = pl.multiple_of(j * KB, KB)
        ki = ik_ref[0, pl.ds(k0, KB), 0:IDX_DH].astype(MXU_DT)
        sc = jnp.zeros((KB, QB), F32)
        for h in range(IDX_HEADS):
            r = _nt_dot(ki, iqb[:, h * IDX_DH:(h + 1) * IDX_DH])
            sc = sc + jnp.maximum(r, 0.0) * w_rows[h:h + 1, :]
        bits = lax.bitcast_convert_type(sc, I32)
        key = jnp.where(bits < 0, bits ^ jnp.int32(0x7FFFFFFF), bits)
        key = jnp.where(k0 + row <= q0 + col, key, jnp.int32(INT_MIN))
        key_s[pl.ds(k0, KB), :] = key
        return carry

    lax.fori_loop(0, nkb, score_body, 0)

    def count(cand, strict):
        def body(j, c8):
            k = key_s[pl.ds(pl.multiple_of(j * KB, KB), KB), :]
            hit = (k > cand) if strict else (k >= cand)
            return c8 + jnp.sum(jnp.where(hit, 1, 0).astype(I32).reshape(KB // SUBLANES, SUBLANES, QB), axis=0)
        c8 = lax.fori_loop(0, nkb, body, jnp.zeros((SUBLANES, QB), I32))
        return jnp.sum(c8, axis=0, keepdims=True)

    thr_s[0:1, :] = jnp.full((1, QB), INT_MIN + 1, I32)
    thr_s[1:2, :] = jnp.full((1, QB), n_sel, I32)

    @pl.when(qb >= n_sel // QB)
    def _():
        c_pos = count(jnp.zeros((1, QB), I32), False)
        prefix = jnp.where(c_pos >= n_sel, 0, jnp.int32(INT_MIN)).astype(I32)

        def bit_body(it, prefix):
            cand = prefix | jnp.left_shift(jnp.int32(1), 30 - it)
            c = count(cand, False)
            return jnp.where(c >= n_sel, cand, prefix)

        prefix = lax.fori_loop(0, 31, bit_body, prefix)
        thr_s[0:1, :] = prefix
        thr_s[1:2, :] = n_sel - count(prefix, True)

    thr = thr_s[0:1, :]
    need = thr_s[1:2, :].astype(F32)

    aqb = (aq_ref[0] * (dh ** -0.5)).astype(MXU_DT)
    acc_s[...] = jnp.zeros(acc_s.shape, F32)
    krow = lax.broadcasted_iota(I32, (KB, KB), 0)
    kcol = lax.broadcasted_iota(I32, (KB, KB), 1)
    tri = jnp.where(kcol < krow, 1.0, 0.0).astype(MXU_DT)

    def att_body(j, carry):
        rc, ms, ls = carry
        k0 = pl.multiple_of(j * KB, KB)
        kb = kv_ref[0, pl.ds(k0, KB), 0:dh].astype(MXU_DT)
        vt = aux_ref[0, AUX_AV:AUX_AV + dh, pl.ds(k0, KB)].astype(MXU_DT)
        key = key_s[pl.ds(k0, KB), :]
        tie = key == thr
        tie_f = jnp.where(tie, 1.0, 0.0)
        before = _dot(tri, tie_f.astype(MXU_DT)) + rc
        sel = (key > thr) | (tie & (before < need))
        new_ms, new_ls = [], []
        for h in range(A_HEADS):
            st = _nt_dot(kb, aqb[:, h * dh:(h + 1) * dh])
            st = jnp.where(sel, st, NEG_BIG)
            m_new = jnp.maximum(ms[h], jnp.max(st, axis=0, keepdims=True))
            p = jnp.where(sel, jnp.exp(st - m_new), 0.0)
            alpha = jnp.exp(ms[h] - m_new)
            new_ms.append(m_new)
            new_ls.append(alpha * ls[h] + jnp.sum(p, axis=0, keepdims=True))
            acc_s[:, h * QB:(h + 1) * QB] = alpha * acc_s[:, h * QB:(h + 1) * QB] + _dot(vt, p.astype(MXU_DT))
        return rc + jnp.sum(tie_f, axis=0, keepdims=True), tuple(new_ms), tuple(new_ls)

    init = (jnp.zeros((1, QB), F32),
            tuple(jnp.full((1, QB), NEG_BIG, F32) for _ in range(A_HEADS)),
            tuple(jnp.zeros((1, QB), F32) for _ in range(A_HEADS)))
    _, _, ls = lax.fori_loop(0, nkb, att_body, init)

    for h in range(A_HEADS):
        o = acc_s[:, h * QB:(h + 1) * QB] / ls[h]
        o = o * lax.rsqrt(jnp.mean(o * o, axis=0, keepdims=True) + EPS)
        o_ref[0, :, h * dh:(h + 1) * dh] = o.T * ga_ref[:, h * dh:(h + 1) * dh]


def _dsa(z, aux, ga, n_sel):
    B, T, _ = z.shape
    width = ga.shape[1]
    dh = width // A_HEADS
    assert n_sel % Q_BLOCK == 0 and T % (2 * Q_BLOCK) == 0 and dh == Z_AV - Z_AK
    kern = functools.partial(_dsa_kernel, n_sel=n_sel, dh=dh)
    return pl.pallas_call(
        kern,
        grid=(B, T // Q_BLOCK),
        in_specs=[pl.BlockSpec((1, Q_BLOCK, width), lambda b_, i: (b_, i, Z_AQ // width)),
                  pl.BlockSpec((1, Q_BLOCK, width), lambda b_, i: (b_, i, Z_IQ // width)),
                  pl.BlockSpec((1, T, LANES), lambda b_, i: (b_, 0, Z_AK // LANES)),
                  pl.BlockSpec((1, T, LANES), lambda b_, i: (b_, 0, Z_IK // LANES)),
                  pl.BlockSpec((1, AUX_ROWS, T), lambda b_, i: (b_, 0, 0)),
                  pl.BlockSpec((1, width), lambda b_, i: (0, 0))],
        out_specs=pl.BlockSpec((1, Q_BLOCK, width), lambda b_, i: (b_, i, 0)),
        out_shape=jax.ShapeDtypeStruct((B, T, width), F32),
        scratch_shapes=[pltpu.VMEM((T, Q_BLOCK), I32),
                        pltpu.VMEM((SUBLANES, Q_BLOCK), I32),
                        pltpu.VMEM((dh, A_HEADS * Q_BLOCK), F32)],
        compiler_params=_params(("parallel", "arbitrary")),
        name="dsa",
    )(z, z, z, z, aux, ga)


def _post_kernel(hm_ref, ha_ref, x_ref, wo_ref, g1_ref, g_ref, sc_ref, sh_ref, wq_ref, sk_ref,
                 x1_ref, h2t_ref, st_ref):
    wm = hm_ref.shape[2]
    y = _dot(hm_ref[0].astype(MXU_DT), wo_ref[0:wm, :]) + _dot(ha_ref[0].astype(MXU_DT), wo_ref[wm:, :])
    x1 = x_ref[0] + g1_ref[0] * y
    x1_ref[0] = x1
    h2 = _modulated_norm(x1, g_ref[...], sc_ref[0], sh_ref[0])
    h2t_ref[...] = h2.T.astype(h2t_ref.dtype)
    qry = _dot(h2.astype(MXU_DT), wq_ref[...]).astype(MXU_DT)
    dk = sk_ref.shape[2]
    for hp in range(sk_ref.shape[0]):
        st_ref[hp] = _nt_dot(sk_ref[hp], qry[:, hp * dk:(hp + 1) * dk])


def _post(hm, ha, x, wo, g1, g, sc, sh, wq, sk, tm):
    B, T, D = x.shape
    nt = T // tm
    wm = hm.shape[2]
    NS, NK, DK = sk.shape
    return pl.pallas_call(
        _post_kernel,
        grid=(B, nt),
        in_specs=[pl.BlockSpec((1, tm, wm), lambda b_, i: (b_, i, 0)),
                  pl.BlockSpec((1, tm, ha.shape[2]), lambda b_, i: (b_, i, 0)),
                  pl.BlockSpec((1, tm, D), lambda b_, i: (b_, i, 0)),
                  pl.BlockSpec(wo.shape, lambda b_, i: (0, 0)),
                  pl.BlockSpec((1, 1, D), lambda b_, i: (b_, 0, 0)),
                  pl.BlockSpec((1, D), lambda b_, i: (0, 0)),
                  pl.BlockSpec((1, 1, D), lambda b_, i: (b_, 0, 0)),
                  pl.BlockSpec((1, 1, D), lambda b_, i: (b_, 0, 0)),
                  pl.BlockSpec(wq.shape, lambda b_, i: (0, 0)),
                  pl.BlockSpec(sk.shape, lambda b_, i: (0, 0, 0))],
        out_specs=[pl.BlockSpec((1, tm, D), lambda b_, i: (b_, i, 0)),
                   pl.BlockSpec((D, tm), lambda b_, i: (0, b_ * nt + i)),
                   pl.BlockSpec((NS, NK, tm), lambda b_, i: (0, 0, b_ * nt + i))],
        out_shape=[jax.ShapeDtypeStruct((B, T, D), F32),
                   jax.ShapeDtypeStruct((D, B * T), MXU_DT),
                   jax.ShapeDtypeStruct((NS, NK, B * T), F32)],
        compiler_params=_params(("parallel", "parallel")),
        name="post",
    )(hm, ha, x, wo, g1, g, sc, sh, wq, sk)


def _top_sorted(cur, n):
    R = cur.shape[0]
    rows = lax.broadcasted_iota(I32, cur.shape, 0)
    vals = []
    for k in range(n):
        mx = jnp.max(cur, axis=0, keepdims=True)
        vals.append(mx)
        if k + 1 < n:
            first = jnp.min(jnp.where(cur == mx, rows, R), axis=0, keepdims=True)
            cur = jnp.where(rows == first, -jnp.inf, cur)
    return vals


def _select_kernel(st_ref, d1_ref, e1_ref, e2_ref, cand_s):
    n_top = P_TOPK + 1
    pairs = [(a, b) for a in range(n_top) for b in range(n_top) if (a + 1) * (b + 1) <= n_top]
    for h in range(P_HEADS):
        s1 = st_ref[2 * h]
        s2 = st_ref[2 * h + 1]
        v1 = _top_sorted(s1, n_top)
        v2 = _top_sorted(s2, n_top)
        cand_s[...] = jnp.full(cand_s.shape, -jnp.inf, F32)
        for r, (a, b) in enumerate(pairs):
            cand_s[r:r + 1, :] = v1[a] + v2[b]
        top = _top_sorted(cand_s[...], n_top)
        thr = 0.5 * (top[P_TOPK - 1] + top[P_TOPK])
        zsum = jnp.zeros_like(thr)
        for k in range(P_TOPK):
            zsum = zsum + jnp.exp(top[k] - top[0])
        d1_ref[h] = thr - s1
        e1_ref[h] = jnp.exp(s1 - v1[0]) / zsum
        e2_ref[h] = jnp.exp(s2 - v2[0])


def _select(st, tl):
    NS, NK, N = st.shape
    n_top = P_TOPK + 1
    n_pairs = sum(1 for a in range(n_top) for b in range(n_top) if (a + 1) * (b + 1) <= n_top)
    cand_rows = -(-n_pairs // SUBLANES) * SUBLANES
    out = jax.ShapeDtypeStruct((P_HEADS, NK, N), F32)
    spec = pl.BlockSpec((P_HEADS, NK, tl), lambda i: (0, 0, i))
    return pl.pallas_call(
        _select_kernel,
        grid=(N // tl,),
        in_specs=[pl.BlockSpec((NS, NK, tl), lambda i: (0, 0, i))],
        out_specs=[spec, spec, spec],
        out_shape=[out, out, out],
        scratch_shapes=[pltpu.VMEM((cand_rows, tl), F32)],
        compiler_params=_params(("parallel",)),
        name="select",
    )(st)


def _gelu(a):
    return 0.5 * a * (1.0 + lax.erf(a * (2.0 ** -0.5)))


def _expert_kernel(h2t_ref, u_ref, vt_ref, st_ref, d1_ref, e1_ref, e2_ref, o_ref, acc_s, p_s, *, lane_tile):
    e = pl.program_id(1)
    te = u_ref.shape[0]
    tn = h2t_ref.shape[1]
    rows_per_step = te // N_KEYS

    @pl.when(e == 0)
    def _():
        acc_s[...] = jnp.zeros(acc_s.shape, F32)

    a_t = _dot(u_ref[...], h2t_ref[...])
    for ii in range(rows_per_step):
        i = e * rows_per_step + ii
        for l0 in range(0, tn, lane_tile):
            g = jnp.zeros((N_KEYS, lane_tile), F32)
            for h in range(P_HEADS):
                d1 = d1_ref[h, pl.ds(i, 1), l0:l0 + lane_tile]
                e1 = e1_ref[h, pl.ds(i, 1), l0:l0 + lane_tile]
                s2 = st_ref[2 * h + 1, :, l0:l0 + lane_tile]
                e2 = e2_ref[h, :, l0:l0 + lane_tile]
                g = g + jnp.where(s2 > d1, e2 * e1, 0.0)
            a_blk = a_t[ii * N_KEYS:(ii + 1) * N_KEYS, l0:l0 + lane_tile]
            p_s[ii * N_KEYS:(ii + 1) * N_KEYS, l0:l0 + lane_tile] = (g * _gelu(a_blk)).astype(p_s.dtype)
    acc_s[...] += _dot(vt_ref[...], p_s[...])

    @pl.when(e == pl.num_programs(1) - 1)
    def _():
        o_ref[...] = acc_s[...]


def _expert(h2t, u, vt, st, d1, e1, e2, tn, te):
    D, N = h2t.shape
    NE = u.shape[0]
    NS, NK, _ = st.shape
    kern = functools.partial(_expert_kernel, lane_tile=min(tn, 2 * LANES))
    hspec = pl.BlockSpec((P_HEADS, NK, tn), lambda t, e: (0, 0, t))
    return pl.pallas_call(
        kern,
        grid=(N // tn, NE // te),
        in_specs=[pl.BlockSpec((D, tn), lambda t, e: (0, t)),
                  pl.BlockSpec((te, D), lambda t, e: (e, 0)),
                  pl.BlockSpec((D, te), lambda t, e: (0, e)),
                  pl.BlockSpec((NS, NK, tn), lambda t, e: (0, 0, t)),
                  hspec, hspec, hspec],
        out_specs=pl.BlockSpec((D, tn), lambda t, e: (0, t)),
        out_shape=jax.ShapeDtypeStruct((D, N), F32),
        scratch_shapes=[pltpu.VMEM((D, tn), F32),
                        pltpu.VMEM((te, tn), MXU_DT)],
        compiler_params=_params(("parallel", "arbitrary")),
        name="expert",
    )(h2t, u, vt, st, d1, e1, e2)


def _final_kernel(x1_ref, pt_ref, g2_ref, gf_ref, o_ref, *, normalize):
    x2 = x1_ref[0] + g2_ref[0] * pt_ref[...].T
    if normalize:
        ms = jnp.mean(x2 * x2, axis=-1, keepdims=True)
        x2 = x2 * lax.rsqrt(ms + EPS) * gf_ref[...]
    o_ref[0] = x2


def _final(x1, pt, g2, gf, tm, normalize):
    B, T, D = x1.shape
    nt = T // tm
    return pl.pallas_call(
        functools.partial(_final_kernel, normalize=normalize),
        grid=(B, nt),
        in_specs=[pl.BlockSpec((1, tm, D), lambda b_, i: (b_, i, 0)),
                  pl.BlockSpec((D, tm), lambda b_, i: (0, b_ * nt + i)),
                  pl.BlockSpec((1, 1, D), lambda b_, i: (b_, 0, 0)),
                  pl.BlockSpec((1, D), lambda b_, i: (0, 0))],
        out_specs=pl.BlockSpec((1, tm, D), lambda b_, i: (b_, i, 0)),
        out_shape=jax.ShapeDtypeStruct((B, T, D), F32),
        compiler_params=_params(("parallel", "parallel")),
        name="final",
    )(x1, pt, g2, gf)


def _pack_in_weights(w_in, b_in):
    D = w_in.shape[0]
    o_mi, o_aq, o_ak, o_av, o_iq, o_ik, o_iw, o_end = 2048, 2056, 2568, 2632, 2696, 3208, 3272, 3280
    order = [(0, o_mi), (o_aq, o_ak), (o_iq, o_ik), (o_ak, o_av), (o_av, o_iq), (o_ik, o_iw),
             (o_mi, o_aq), (o_iw, o_end)]
    n_used = sum(hi - lo for lo, hi in order)
    w = jnp.concatenate([w_in[:, lo:hi] for lo, hi in order] + [jnp.zeros((D, Z_COLS - n_used), F32)], axis=1)
    b = jnp.concatenate([b_in[lo:hi] for lo, hi in order] + [jnp.zeros((Z_COLS - n_used,), F32)])
    aux_order = [(o_av, o_iq), (o_mi, o_aq), (o_iw, o_end)]
    n_aux = sum(hi - lo for lo, hi in aux_order)
    wt = jnp.concatenate([w_in[:, lo:hi] for lo, hi in aux_order] + [jnp.zeros((D, AUX_ROWS - n_aux), F32)], axis=1).T
    bt = jnp.concatenate([b_in[lo:hi] for lo, hi in aux_order] + [jnp.zeros((AUX_ROWS - n_aux,), F32)])
    return w.astype(MXU_DT), b.reshape(1, Z_COLS), wt.astype(MXU_DT), bt.reshape(AUX_ROWS, 1)


def _layer(x, ada, norm1_g, w_in, b_in, conv_qk, out_norm_m, out_norm_a, w_out, norm2_g,
           w_peer_q, peer_sub_keys, peer_u, peer_v, *, tm, mrows, tl, tn, te):
    B, T, D = x.shape
    sh1, sc1, g1, sh2, sc2, g2 = [a.reshape(B, 1, D) for a in jnp.split(ada, 6, axis=-1)]
    w, b, wt, bt = _pack_in_weights(w_in, b_in)
    z, aux = _inproj(x, norm1_g.reshape(1, D), sc1, sh1, w, b, wt, bt, tm)
    hm = _mlstm(z, aux, conv_qk, out_norm_m.reshape(1, -1), mrows)
    ha = _dsa(z, aux, out_norm_a.reshape(1, -1), min(TOPK_MAX, T // 4))
    sk = peer_sub_keys.reshape(2 * P_HEADS, N_KEYS, -1).astype(MXU_DT)
    x1, h2t, st = _post(hm, ha, x, w_out.astype(MXU_DT), g1, norm2_g.reshape(1, D), sc2, sh2,
                        w_peer_q.astype(MXU_DT), sk, tm)
    d1, e1, e2 = _select(st, tl)
    pt = _expert(h2t, peer_u.astype(MXU_DT), peer_v.T.astype(MXU_DT), st, d1, e1, e2, tn, te)
    return x1, pt, g2


def kernel(x, c, w_ada, b_ada, norm1_g, w_in, b_in, conv_qk, out_norm_m, out_norm_a, w_out, norm2_g,
           w_peer_q, peer_sub_keys, peer_u, peer_v, final_g):
    B, T, D = x.shape
    depth = w_ada.shape[0]
    tm = min(512, T)
    for l in range(depth):
        ada = _ada(c, w_ada[l], b_ada[l])
        x1, pt, g2 = _layer(x, ada, norm1_g[l], w_in[l], b_in[l], conv_qk[l], out_norm_m[l], out_norm_a[l],
                            w_out[l], norm2_g[l], w_peer_q[l], peer_sub_keys[l], peer_u[l], peer_v[l],
                            tm=tm, mrows=min(256, T), tl=256, tn=512, te=512)
        x = _final(x1, pt, g2, final_g.reshape(1, D), tm, normalize=(l + 1 == depth))
    return x
```

```python
import functools

import jax
import jax.numpy as jnp
from jax import lax
from jax.experimental import pallas as pl
from jax.experimental.pallas import tpu as pltpu

F32 = jnp.float32
I32 = jnp.int32
MXU_DT = jnp.bfloat16
EPS = 1e-6

M_HEADS = 4
CONV_W = 4
CHUNK = 64
A_HEADS = 8
IDX_HEADS = 8
IDX_DH = 64
TOPK_MAX = 256
Q_BLOCK = 128
P_HEADS = 8
N_KEYS = 128
P_TOPK = 16

LANES = 128
SUBLANES = 8
VMEM_LIMIT = 56 * 1024 * 1024

INT_MIN = -(2 ** 31)
NEG_BIG = -1e30


def _params(sem, vmem=VMEM_LIMIT):
    return pltpu.CompilerParams(dimension_semantics=sem, vmem_limit_bytes=vmem)


def _nt_dot(a, b):
    return lax.dot_general(a, b, (((1,), (1,)), ((), ())), preferred_element_type=F32)


def _dot(a, b):
    return jnp.dot(a, b, preferred_element_type=F32)


def _ada_kernel(c_ref, w_ref, b_ref, o_ref):
    c = c_ref[...]
    sc = c * (1.0 / (1.0 + jnp.exp(-c)))
    o_ref[...] = jnp.dot(sc, w_ref[...], preferred_element_type=F32,
                         precision=lax.Precision.HIGHEST) + b_ref[...]


def _ada(c, w, b):
    B, D = c.shape
    N = w.shape[1]
    return pl.pallas_call(
        _ada_kernel,
        grid=(N // D,),
        in_specs=[pl.BlockSpec((B, D), lambda j: (0, 0)),
                  pl.BlockSpec((D, D), lambda j: (0, j)),
                  pl.BlockSpec((1, D), lambda j: (0, j))],
        out_specs=pl.BlockSpec((B, D), lambda j: (0, j)),
        out_shape=jax.ShapeDtypeStruct((B, N), F32),
        compiler_params=_params(("arbitrary",)),
        name="ada",
    )(c, w, b.reshape(1, N))


def _modulated_norm(x, g, sc, sh):
    ms = jnp.mean(x * x, axis=-1, keepdims=True)
    return (x * lax.rsqrt(ms + EPS) * g) * (1.0 + sc) + sh


def _inproj_kernel(x_ref, g_ref, sc_ref, sh_ref, w_ref, b_ref, wt_ref, bt_ref, z_ref, aux_ref, *, col_chunk):
    h = _modulated_norm(x_ref[0], g_ref[...], sc_ref[0], sh_ref[0])
    hb = h.astype(MXU_DT)
    n_cols = w_ref.shape[1]
    for c0 in range(0, n_cols, col_chunk):
        z_ref[0, :, c0:c0 + col_chunk] = _dot(hb, w_ref[:, c0:c0 + col_chunk]) + b_ref[:, c0:c0 + col_chunk]
    aux_ref[0] = _nt_dot(wt_ref[...], hb) + bt_ref[...]


def _inproj(x, g, sc, sh, w, b, wt, bt, tm):
    B, T, D = x.shape
    NZ = w.shape[1]
    NA = wt.shape[0]
    return pl.pallas_call(
        functools.partial(_inproj_kernel, col_chunk=256),
        grid=(B, T // tm),
        in_specs=[pl.BlockSpec((1, tm, D), lambda b_, i: (b_, i, 0)),
                  pl.BlockSpec((1, D), lambda b_, i: (0, 0)),
                  pl.BlockSpec((1, 1, D), lambda b_, i: (b_, 0, 0)),
                  pl.BlockSpec((1, 1, D), lambda b_, i: (b_, 0, 0)),
                  pl.BlockSpec((D, NZ), lambda b_, i: (0, 0)),
                  pl.BlockSpec((1, NZ), lambda b_, i: (0, 0)),
                  pl.BlockSpec((NA, D), lambda b_, i: (0, 0)),
                  pl.BlockSpec((NA, 1), lambda b_, i: (0, 0))],
        out_specs=[pl.BlockSpec((1, tm, NZ), lambda b_, i: (b_, i, 0)),
                   pl.BlockSpec((1, NA, tm), lambda b_, i: (b_, 0, i))],
        out_shape=[jax.ShapeDtypeStruct((B, T, NZ), F32),
                   jax.ShapeDtypeStruct((B, NA, T), F32)],
        compiler_params=_params(("parallel", "parallel")),
        name="inproj",
    )(x, g, sc, sh, w, b, wt, bt)


Z_MQ, Z_MK, Z_MV, Z_MO, Z_AQ, Z_IQ = 0, 512, 1024, 1536, 2048, 2560
Z_AK, Z_AV, Z_IK, Z_MI, Z_MF, Z_IW, Z_COLS = 3072, 3136, 3200, 3264, 3268, 3272, 3328
AUX_AV, AUX_MI, AUX_MF, AUX_IW, AUX_ROWS = 0, 64, 68, 72, 128


def _log_sigmoid(f):
    return jnp.minimum(f, 0.0) - jnp.log1p(jnp.exp(-jnp.abs(f)))


def _mlstm_kernel(qk_ref, vo_ref, gz_ref, aux_ref, cw_ref, gm_ref, o_ref, hist, c_s, n_s, m_s, *, rows, dh):
    i = pl.program_id(1)
    halo = SUBLANES
    L = CHUNK

    @pl.when(i == 0)
    def _():
        hist[0:halo, :] = jnp.zeros((halo, hist.shape[1]), F32)
        c_s[...] = jnp.zeros(c_s.shape, F32)
        n_s[...] = jnp.zeros(n_s.shape, F32)
        m_s[...] = jnp.zeros(m_s.shape, F32)

    hist[halo:halo + rows, :] = qk_ref[0]
    acc = hist[halo - 3:halo - 3 + rows, :] * cw_ref[0:1, :]
    for j in range(1, CONV_W):
        acc = acc + hist[halo - 3 + j:halo - 3 + j + rows, :] * cw_ref[j:j + 1, :]
    hist[0:halo, :] = hist[rows:rows + halo, :]
    qk = acc * (1.0 / (1.0 + jnp.exp(-acc)))
    width = M_HEADS * dh

    tt = lax.broadcasted_iota(I32, (L, L), 0)
    ss = lax.broadcasted_iota(I32, (L, L), 1)
    causal = ss <= tt

    for c in range(rows // L):
        r0 = c * L
        for h in range(M_HEADS):
            qh = qk[r0:r0 + L, h * dh:(h + 1) * dh] * (dh ** -0.5)
            kh = qk[r0:r0 + L, width + h * dh:width + (h + 1) * dh]
            vh = vo_ref[0, r0:r0 + L, h * dh:(h + 1) * dh]
            oh = vo_ref[0, r0:r0 + L, width + h * dh:width + (h + 1) * dh]
            gcol = Z_MI - Z_IK
            ig_col = gz_ref[0, r0:r0 + L, gcol + h:gcol + h + 1]
            fg_col = _log_sigmoid(gz_ref[0, r0:r0 + L, gcol + M_HEADS + h:gcol + M_HEADS + h + 1])
            ig_row = aux_ref[0, AUX_MI + h:AUX_MI + h + 1, r0:r0 + L]
            fg_row = _log_sigmoid(aux_ref[0, AUX_MF + h:AUX_MF + h + 1, r0:r0 + L])

            b_col = jnp.sum(jnp.where(causal, fg_row, 0.0), axis=1, keepdims=True)
            b_row = jnp.sum(jnp.where(tt <= ss, fg_col, 0.0), axis=0, keepdims=True)
            b_last = jnp.sum(fg_row, axis=1, keepdims=True)
            log_d = jnp.where(causal, b_col - b_row + ig_row, -jnp.inf)
            max_d = jnp.max(log_d, axis=1, keepdims=True)
            m_loc = jnp.max(b_last - b_row + ig_row, axis=1, keepdims=True)
            w_col = jnp.exp(b_last - b_col + ig_col - m_loc)
            kw = kh * w_col
            kv = _dot(kw.T.astype(MXU_DT), vh.astype(MXU_DT))
            ks = jnp.sum(kw, axis=0, keepdims=True)

            c0 = c_s[h]
            n0 = n_s[h:h + 1, :]
            m0 = m_s[h:h + 1, 0:1]
            m_inter = b_col + m0
            m_t = jnp.maximum(m_inter, max_d)
            w_inter = jnp.exp(m_inter - m_t)
            qb = qh.astype(MXU_DT)
            s = _nt_dot(qb, kh.astype(MXU_DT)) * jnp.exp(log_d - m_t)
            num = _dot(s.astype(MXU_DT), vh.astype(MXU_DT)) + w_inter * _dot(qb, c0.astype(MXU_DT))
            den = jnp.sum(s, axis=1, keepdims=True) + w_inter * jnp.sum(qh * n0, axis=1, keepdims=True)
            hh = num / jnp.maximum(jnp.abs(den), jnp.exp(-m_t))

            m_new = jnp.maximum(b_last + m0, m_loc)
            dec = jnp.exp(b_last + m0 - m_new)
            inj = jnp.exp(m_loc - m_new)
            c_s[h] = dec * c0 + inj * kv
            n_s[h:h + 1, :] = dec * n0 + inj * ks
            m_s[h:h + 1, :] = jnp.broadcast_to(m_new, (1, m_s.shape[1]))

            hn = hh * lax.rsqrt(jnp.mean(hh * hh, axis=-1, keepdims=True) + EPS) * gm_ref[:, h * dh:(h + 1) * dh]
            o_ref[0, r0:r0 + L, h * dh:(h + 1) * dh] = hn * (1.0 / (1.0 + jnp.exp(-oh)))


def _mlstm(z, aux, conv_w, gm, rows):
    B, T, _ = z.shape
    width = gm.shape[1]
    dh = width // M_HEADS
    kern = functools.partial(_mlstm_kernel, rows=rows, dh=dh)
    return pl.pallas_call(
        kern,
        grid=(B, T // rows),
        in_specs=[pl.BlockSpec((1, rows, 2 * width), lambda b_, i: (b_, i, 0)),
                  pl.BlockSpec((1, rows, 2 * width), lambda b_, i: (b_, i, 1)),
                  pl.BlockSpec((1, rows, LANES), lambda b_, i: (b_, i, Z_IK // LANES)),
                  pl.BlockSpec((1, AUX_ROWS, rows), lambda b_, i: (b_, 0, i)),
                  pl.BlockSpec((CONV_W, 2 * width), lambda b_, i: (0, 0)),
                  pl.BlockSpec((1, width), lambda b_, i: (0, 0))],
        out_specs=pl.BlockSpec((1, rows, width), lambda b_, i: (b_, i, 0)),
        out_shape=jax.ShapeDtypeStruct((B, T, width), F32),
        scratch_shapes=[pltpu.VMEM((rows + SUBLANES, 2 * width), F32),
                        pltpu.VMEM((M_HEADS, dh, dh), F32),
                        pltpu.VMEM((SUBLANES, dh), F32),
                        pltpu.VMEM((SUBLANES, LANES), F32)],
        compiler_params=_params(("parallel", "arbitrary")),
        name="mlstm",
    )(z, z, z, aux, conv_w, gm)


def _dsa_kernel(aq_ref, iq_ref, kv_ref, ik_ref, aux_ref, ga_ref, o_ref, key_s, thr_s, acc_s, tri_s, *, n_sel, dh):
    qb = pl.program_id(1)
    QB = Q_BLOCK
    KB = 2 * QB
    nkb = (qb + 2) // 2
    q0 = pl.multiple_of(qb * QB, QB)
    idx_scale = (IDX_DH ** -0.5) * (IDX_HEADS ** -0.5)
    row = lax.broadcasted_iota(I32, (KB, QB), 0)
    col = lax.broadcasted_iota(I32, (KB, QB), 1)

    iqb = iq_ref[0].astype(MXU_DT)
    w_rows = aux_ref[0, AUX_IW:AUX_IW + IDX_HEADS, pl.ds(q0, QB)] * idx_scale

    def score_body(j, carry):
        k0 = pl.multiple_of(j * KB, KB)
        ki = ik_ref[0, pl.ds(k0, KB), 0:IDX_DH].astype(MXU_DT)
        sc = jnp.zeros((KB, QB), F32)
        for h in range(IDX_HEADS):
            r = _nt_dot(ki, iqb[:, h * IDX_DH:(h + 1) * IDX_DH])
            sc = sc + jnp.maximum(r, 0.0) * w_rows[h:h + 1, :]
        key_s[pl.ds(k0, KB), :] = jnp.where(k0 + row <= q0 + col, sc, -jnp.inf)
        return carry

    lax.fori_loop(0, nkb, score_body, 0)

    def count(cand, strict):
        def body(j, c8):
            k = key_s[pl.ds(pl.multiple_of(j * KB, KB), KB), :]
            hit = (k > cand) if strict else (k >= cand)
            return c8 + jnp.sum(jnp.where(hit, 1, 0).astype(I32).reshape(KB // SUBLANES, SUBLANES, QB), axis=0)
        c8 = lax.fori_loop(0, nkb, body, jnp.zeros((SUBLANES, QB), I32))
        return jnp.sum(c8, axis=0, keepdims=True)

    def key_to_float(key):
        return lax.bitcast_convert_type(jnp.where(key < 0, key ^ jnp.int32(0x7FFFFFFF), key), F32)

    thr_s[0:1, :] = jnp.full((1, QB), -jnp.inf, F32)
    thr_s[1:2, :] = jnp.zeros((1, QB), F32)

    @pl.when(qb >= n_sel // QB)
    def _():
        c_pos = count(jnp.zeros((1, QB), F32), False)
        prefix = jnp.where(c_pos >= n_sel, 0, jnp.int32(INT_MIN)).astype(I32)

        def bit_body(it, prefix):
            cand = prefix | jnp.left_shift(jnp.int32(1), 30 - it)
            c = count(key_to_float(cand), False)
            return jnp.where(c >= n_sel, cand, prefix)

        thr_f = key_to_float(lax.fori_loop(0, 31, bit_body, prefix))
        thr_s[0:1, :] = thr_f
        thr_s[1:2, :] = (n_sel - count(thr_f, True)).astype(F32)

    thr = thr_s[0:1, :]
    need = thr_s[1:2, :]

    aqb = (aq_ref[0] * (dh ** -0.5)).astype(MXU_DT)
    acc_s[...] = jnp.zeros(acc_s.shape, F32)
    krow = lax.broadcasted_iota(I32, (KB, KB), 0)
    kcol = lax.broadcasted_iota(I32, (KB, KB), 1)
    tri_s[...] = jnp.where(kcol < krow, 1.0, 0.0).astype(MXU_DT)

    def att_body(j, carry):
        rc, ms, ls = carry
        k0 = pl.multiple_of(j * KB, KB)
        kb = kv_ref[0, pl.ds(k0, KB), 0:dh].astype(MXU_DT)
        vt = aux_ref[0, AUX_AV:AUX_AV + dh, pl.ds(k0, KB)].astype(MXU_DT)
        key = key_s[pl.ds(k0, KB), :]
        tie = key == thr
        tie_f = jnp.where(tie, 1.0, 0.0)
        before = _dot(tri_s[...], tie_f.astype(MXU_DT)) + rc
        sel = (key > thr) | (tie & (before < need))
        bias = jnp.where(sel, 0.0, -jnp.inf)
        new_ms, new_ls = [], []
        for h in range(A_HEADS):
            st = _nt_dot(kb, aqb[:, h * dh:(h + 1) * dh]) + bias
            m_new = jnp.maximum(ms[h], jnp.max(st, axis=0, keepdims=True))
            p = jnp.exp(st - m_new)
            alpha = jnp.exp(ms[h] - m_new)
            new_ms.append(m_new)
            new_ls.append(alpha * ls[h] + jnp.sum(p, axis=0, keepdims=True))
            acc_s[:, h * QB:(h + 1) * QB] = alpha * acc_s[:, h * QB:(h + 1) * QB] + _dot(vt, p.astype(MXU_DT))
        return rc + jnp.sum(tie_f, axis=0, keepdims=True), tuple(new_ms), tuple(new_ls)

    init = (jnp.zeros((1, QB), F32),
            tuple(jnp.full((1, QB), NEG_BIG, F32) for _ in range(A_HEADS)),
            tuple(jnp.zeros((1, QB), F32) for _ in range(A_HEADS)))
    _, _, ls = lax.fori_loop(0, nkb, att_body, init)

    for h in range(A_HEADS):
        o = acc_s[:, h * QB:(h + 1) * QB] / ls[h]
        o = o * lax.rsqrt(jnp.mean(o * o, axis=0, keepdims=True) + EPS)
        o_ref[0, :, h * dh:(h + 1) * dh] = o.T * ga_ref[:, h * dh:(h + 1) * dh]


def _dsa(z, aux, ga, n_sel):
    B, T, _ = z.shape
    width = ga.shape[1]
    dh = width // A_HEADS
    assert n_sel % Q_BLOCK == 0 and T % (2 * Q_BLOCK) == 0 and dh == Z_AV - Z_AK
    kern = functools.partial(_dsa_kernel, n_sel=n_sel, dh=dh)
    return pl.pallas_call(
        kern,
        grid=(B, T // Q_BLOCK),
        in_specs=[pl.BlockSpec((1, Q_BLOCK, width), lambda b_, i: (b_, i, Z_AQ // width)),
                  pl.BlockSpec((1, Q_BLOCK, width), lambda b_, i: (b_, i, Z_IQ // width)),
                  pl.BlockSpec((1, T, LANES), lambda b_, i: (b_, 0, Z_AK // LANES)),
                  pl.BlockSpec((1, T, LANES), lambda b_, i: (b_, 0, Z_IK // LANES)),
                  pl.BlockSpec((1, AUX_ROWS, T), lambda b_, i: (b_, 0, 0)),
                  pl.BlockSpec((1, width), lambda b_, i: (0, 0))],
        out_specs=pl.BlockSpec((1, Q_BLOCK, width), lambda b_, i: (b_, i, 0)),
        out_shape=jax.ShapeDtypeStruct((B, T, width), F32),
        scratch_shapes=[pltpu.VMEM((T, Q_BLOCK), F32),
                        pltpu.VMEM((SUBLANES, Q_BLOCK), F32),
                        pltpu.VMEM((dh, A_HEADS * Q_BLOCK), F32),
                        pltpu.VMEM((2 * Q_BLOCK, 2 * Q_BLOCK), MXU_DT)],
        compiler_params=_params(("parallel", "arbitrary")),
        name="dsa",
    )(z, z, z, z, aux, ga)


def _post_kernel(hm_ref, ha_ref, x_ref, wo_ref, g1_ref, g_ref, sc_ref, sh_ref, wq_ref, sk_ref,
                 x1_ref, h2t_ref, st_ref):
    wm = hm_ref.shape[2]
    y = _dot(hm_ref[0].astype(MXU_DT), wo_ref[0:wm, :]) + _dot(ha_ref[0].astype(MXU_DT), wo_ref[wm:, :])
    x1 = x_ref[0] + g1_ref[0] * y
    x1_ref[0] = x1
    h2 = _modulated_norm(x1, g_ref[...], sc_ref[0], sh_ref[0])
    h2t_ref[...] = h2.T.astype(h2t_ref.dtype)
    qry = _dot(h2.astype(MXU_DT), wq_ref[...]).astype(MXU_DT)
    dk = sk_ref.shape[2]
    for hp in range(sk_ref.shape[0]):
        st_ref[hp] = _nt_dot(sk_ref[hp], qry[:, hp * dk:(hp + 1) * dk])


def _post(hm, ha, x, wo, g1, g, sc, sh, wq, sk, tm):
    B, T, D = x.shape
    nt = T // tm
    wm = hm.shape[2]
    NS, NK, DK = sk.shape
    return pl.pallas_call(
        _post_kernel,
        grid=(B, nt),
        in_specs=[pl.BlockSpec((1, tm, wm), lambda b_, i: (b_, i, 0)),
                  pl.BlockSpec((1, tm, ha.shape[2]), lambda b_, i: (b_, i, 0)),
                  pl.BlockSpec((1, tm, D), lambda b_, i: (b_, i, 0)),
                  pl.BlockSpec(wo.shape, lambda b_, i: (0, 0)),
                  pl.BlockSpec((1, 1, D), lambda b_, i: (b_, 0, 0)),
                  pl.BlockSpec((1, D), lambda b_, i: (0, 0)),
                  pl.BlockSpec((1, 1, D), lambda b_, i: (b_, 0, 0)),
                  pl.BlockSpec((1, 1, D), lambda b_, i: (b_, 0, 0)),
                  pl.BlockSpec(wq.shape, lambda b_, i: (0, 0)),
                  pl.BlockSpec(sk.shape, lambda b_, i: (0, 0, 0))],
        out_specs=[pl.BlockSpec((1, tm, D), lambda b_, i: (b_, i, 0)),
                   pl.BlockSpec((D, tm), lambda b_, i: (0, b_ * nt + i)),
                   pl.BlockSpec((NS, NK, tm), lambda b_, i: (0, 0, b_ * nt + i))],
        out_shape=[jax.ShapeDtypeStruct((B, T, D), F32),
                   jax.ShapeDtypeStruct((D, B * T), MXU_DT),
                   jax.ShapeDtypeStruct((NS, NK, B * T), F32)],
        compiler_params=_params(("parallel", "parallel")),
        name="post",
    )(hm, ha, x, wo, g1, g, sc, sh, wq, sk)


def _top_sorted(cur, n):
    R = cur.shape[0]
    rows = lax.broadcasted_iota(I32, cur.shape, 0)
    vals = []
    for k in range(n):
        mx = jnp.max(cur, axis=0, keepdims=True)
        vals.append(mx)
        if k + 1 < n:
            first = jnp.min(jnp.where(cur == mx, rows, R), axis=0, keepdims=True)
            cur = jnp.where(rows == first, -jnp.inf, cur)
    return vals


def _select_kernel(st_ref, d1_ref, e1_ref, e2_ref, cand_s):
    n_top = P_TOPK + 1
    pairs = [(a, b) for a in range(n_top) for b in range(n_top) if (a + 1) * (b + 1) <= n_top]
    for h in range(P_HEADS):
        s1 = st_ref[2 * h]
        s2 = st_ref[2 * h + 1]
        v1 = _top_sorted(s1, n_top)
        v2 = _top_sorted(s2, n_top)
        cand_s[...] = jnp.full(cand_s.shape, -jnp.inf, F32)
        for r, (a, b) in enumerate(pairs):
            cand_s[r:r + 1, :] = v1[a] + v2[b]
        top = _top_sorted(cand_s[...], n_top)
        thr = 0.5 * (top[P_TOPK - 1] + top[P_TOPK])
        zsum = jnp.zeros_like(thr)
        for k in range(P_TOPK):
            zsum = zsum + jnp.exp(top[k] - top[0])
        d1_ref[h] = jnp.exp(thr - s1 - v2[0])
        e1_ref[h] = jnp.exp(s1 - v1[0]) / zsum
        e2_ref[h] = jnp.exp(s2 - v2[0])


def _select(st, tl):
    NS, NK, N = st.shape
    n_top = P_TOPK + 1
    n_pairs = sum(1 for a in range(n_top) for b in range(n_top) if (a + 1) * (b + 1) <= n_top)
    cand_rows = -(-n_pairs // SUBLANES) * SUBLANES
    out = jax.ShapeDtypeStruct((P_HEADS, NK, N), F32)
    spec = pl.BlockSpec((P_HEADS, NK, tl), lambda i: (0, 0, i))
    return pl.pallas_call(
        _select_kernel,
        grid=(N // tl,),
        in_specs=[pl.BlockSpec((NS, NK, tl), lambda i: (0, 0, i))],
        out_specs=[spec, spec, spec],
        out_shape=[out, out, out],
        scratch_shapes=[pltpu.VMEM((cand_rows, tl), F32)],
        compiler_params=_params(("parallel",)),
        name="select",
    )(st)


def _gelu(a):
    return 0.5 * a * (1.0 + lax.erf(a * (2.0 ** -0.5)))


def _expert_kernel(h2t_ref, u_ref, vt_ref, d1_ref, e1_ref, e2_ref, o_ref, acc_s, p_s, *, lane_tile):
    e = pl.program_id(1)
    te = u_ref.shape[0]
    tn = h2t_ref.shape[1]
    rows_per_step = te // N_KEYS

    @pl.when(e == 0)
    def _():
        acc_s[...] = jnp.zeros(acc_s.shape, F32)

    a_t = _dot(u_ref[...], h2t_ref[...])
    for ii in range(rows_per_step):
        i = e * rows_per_step + ii
        for l0 in range(0, tn, lane_tile):
            d_rows = [d1_ref[h, pl.ds(i, 1), l0:l0 + lane_tile] for h in range(P_HEADS)]
            e_rows = [e1_ref[h, pl.ds(i, 1), l0:l0 + lane_tile] for h in range(P_HEADS)]
            for c0 in range(0, lane_tile, LANES):
                g = jnp.zeros((N_KEYS, LANES), F32)
                for h in range(P_HEADS):
                    e2 = e2_ref[h, :, l0 + c0:l0 + c0 + LANES]
                    g = g + jnp.where(e2 > d_rows[h][:, c0:c0 + LANES], e2 * e_rows[h][:, c0:c0 + LANES], 0.0)
                a_blk = a_t[ii * N_KEYS:(ii + 1) * N_KEYS, l0 + c0:l0 + c0 + LANES]
                p_s[ii * N_KEYS:(ii + 1) * N_KEYS, l0 + c0:l0 + c0 + LANES] = (g * _gelu(a_blk)).astype(p_s.dtype)
    acc_s[...] += _dot(vt_ref[...], p_s[...])

    @pl.when(e == pl.num_programs(1) - 1)
    def _():
        o_ref[...] = acc_s[...]


def _expert(h2t, u, vt, d1, e1, e2, tn, te):
    D, N = h2t.shape
    NE = u.shape[0]
    NK = d1.shape[1]
    kern = functools.partial(_expert_kernel, lane_tile=min(tn, 2 * LANES))
    hspec = pl.BlockSpec((P_HEADS, NK, tn), lambda t, e: (0, 0, t))
    return pl.pallas_call(
        kern,
        grid=(N // tn, NE // te),
        in_specs=[pl.BlockSpec((D, tn), lambda t, e: (0, t)),
                  pl.BlockSpec((te, D), lambda t, e: (e, 0)),
                  pl.BlockSpec((D, te), lambda t, e: (0, e)),
                  hspec, hspec, hspec],
        out_specs=pl.BlockSpec((D, tn), lambda t, e: (0, t)),
        out_shape=jax.ShapeDtypeStruct((D, N), F32),
        scratch_shapes=[pltpu.VMEM((D, tn), F32),
                        pltpu.VMEM((te, tn), MXU_DT)],
        compiler_params=_params(("parallel", "arbitrary")),
        name="expert",
    )(h2t, u, vt, d1, e1, e2)


def _final_kernel(x1_ref, pt_ref, g2_ref, gf_ref, o_ref, *, normalize):
    x2 = x1_ref[0] + g2_ref[0] * pt_ref[...].T
    if normalize:
        ms = jnp.mean(x2 * x2, axis=-1, keepdims=True)
        x2 = x2 * lax.rsqrt(ms + EPS) * gf_ref[...]
    o_ref[0] = x2


def _final(x1, pt, g2, gf, tm, normalize):
    B, T, D = x1.shape
    nt = T // tm
    return pl.pallas_call(
        functools.partial(_final_kernel, normalize=normalize),
        grid=(B, nt),
        in_specs=[pl.BlockSpec((1, tm, D), lambda b_, i: (b_, i, 0)),
                  pl.BlockSpec((D, tm), lambda b_, i: (0, b_ * nt + i)),
                  pl.BlockSpec((1, 1, D), lambda b_, i: (b_, 0, 0)),
                  pl.BlockSpec((1, D), lambda b_, i: (0, 0))],
        out_specs=pl.BlockSpec((1, tm, D), lambda b_, i: (b_, i, 0)),
        out_shape=jax.ShapeDtypeStruct((B, T, D), F32),
        compiler_params=_params(("parallel", "parallel")),
        name="final",
    )(x1, pt, g2, gf)


def _pack_in_weights(w_in, b_in):
    D = w_in.shape[0]
    o_mi, o_aq, o_ak, o_av, o_iq, o_ik, o_iw, o_end = 2048, 2056, 2568, 2632, 2696, 3208, 3272, 3280
    order = [(0, o_mi), (o_aq, o_ak), (o_iq, o_ik), (o_ak, o_av), (o_av, o_iq), (o_ik, o_iw),
             (o_mi, o_aq), (o_iw, o_end)]
    n_used = sum(hi - lo for lo, hi in order)
    w = jnp.concatenate([w_in[:, lo:hi] for lo, hi in order] + [jnp.zeros((D, Z_COLS - n_used), F32)], axis=1)
    b = jnp.concatenate([b_in[lo:hi] for lo, hi in order] + [jnp.zeros((Z_COLS - n_used,), F32)])
    aux_order = [(o_av, o_iq), (o_mi, o_aq), (o_iw, o_end)]
    n_aux = sum(hi - lo for lo, hi in aux_order)
    wt = jnp.concatenate([w_in[:, lo:hi] for lo, hi in aux_order] + [jnp.zeros((D, AUX_ROWS - n_aux), F32)], axis=1).T
    bt = jnp.concatenate([b_in[lo:hi] for lo, hi in aux_order] + [jnp.zeros((AUX_ROWS - n_aux,), F32)])
    return w.astype(MXU_DT), b.reshape(1, Z_COLS), wt.astype(MXU_DT), bt.reshape(AUX_ROWS, 1)


def _layer(x, ada, norm1_g, w_in, b_in, conv_qk, out_norm_m, out_norm_a, w_out, norm2_g,
           w_peer_q, peer_sub_keys, peer_u, peer_v, *, tm, mrows, tl, tn, te):
    B, T, D = x.shape
    sh1, sc1, g1, sh2, sc2, g2 = [a.reshape(B, 1, D) for a in jnp.split(ada, 6, axis=-1)]
    w, b, wt, bt = _pack_in_weights(w_in, b_in)
    z, aux = _inproj(x, norm1_g.reshape(1, D), sc1, sh1, w, b, wt, bt, tm)
    hm = _mlstm(z, aux, conv_qk, out_norm_m.reshape(1, -1), mrows)
    ha = _dsa(z, aux, out_norm_a.reshape(1, -1), min(TOPK_MAX, T // 4))
    sk = peer_sub_keys.reshape(2 * P_HEADS, N_KEYS, -1).astype(MXU_DT)
    x1, h2t, st = _post(hm, ha, x, w_out.astype(MXU_DT), g1, norm2_g.reshape(1, D), sc2, sh2,
                        w_peer_q.astype(MXU_DT), sk, tm)
    d1, e1, e2 = _select(st, tl)
    pt = _expert(h2t, peer_u.astype(MXU_DT), peer_v.T.astype(MXU_DT), d1, e1, e2, tn, te)
    return x1, pt, g2


def kernel(x, c, w_ada, b_ada, norm1_g, w_in, b_in, conv_qk, out_norm_m, out_norm_a, w_out, norm2_g,
           w_peer_q, peer_sub_keys, peer_u, peer_v, final_g):
    B, T, D = x.shape
    depth = w_ada.shape[0]
    tm = min(512, T)
    for l in range(depth):
        ada = _ada(c, w_ada[l], b_ada[l])
        x1, pt, g2 = _layer(x, ada, norm1_g[l], w_in[l], b_in[l], conv_qk[l], out_norm_m[l], out_norm_a[l],
                            w_out[l], norm2_g[l], w_peer_q[l], peer_sub_keys[l], peer_u[l], peer_v[l],
                            tm=tm, mrows=min(256, T), tl=256, tn=512, te=512)
        x = _final(x1, pt, g2, final_g.reshape(1, D), tm, normalize=(l + 1 == depth))
    return x
```

```python
import functools

import jax
import jax.numpy as jnp
from jax import lax
from jax.experimental import pallas as pl
from jax.experimental.pallas import tpu as pltpu

F32 = jnp.float32
I32 = jnp.int32
MXU_DT = jnp.bfloat16
EPS = 1e-6

M_HEADS = 4
CONV_W = 4
CHUNK = 64
A_HEADS = 8
IDX_HEADS = 8
IDX_DH = 64
TOPK_MAX = 256
Q_BLOCK = 128
P_HEADS = 8
N_KEYS = 128
P_TOPK = 16

LANES = 128
SUBLANES = 8
VMEM_LIMIT = 56 * 1024 * 1024

INT_MIN = -(2 ** 31)
NEG_BIG = -1e30


def _params(sem, vmem=VMEM_LIMIT):
    return pltpu.CompilerParams(dimension_semantics=sem, vmem_limit_bytes=vmem)


def _nt_dot(a, b):
    return lax.dot_general(a, b, (((1,), (1,)), ((), ())), preferred_element_type=F32)


def _dot(a, b):
    return jnp.dot(a, b, preferred_element_type=F32)


def _ada_kernel(c_ref, w_ref, b_ref, o_ref):
    c = c_ref[...]
    sc = c * (1.0 / (1.0 + jnp.exp(-c)))
    o_ref[...] = jnp.dot(sc, w_ref[...], preferred_element_type=F32,
                         precision=lax.Precision.HIGHEST) + b_ref[...]


def _ada(c, w, b):
    B, D = c.shape
    N = w.shape[1]
    return pl.pallas_call(
        _ada_kernel,
        grid=(N // D,),
        in_specs=[pl.BlockSpec((B, D), lambda j: (0, 0)),
                  pl.BlockSpec((D, D), lambda j: (0, j)),
                  pl.BlockSpec((1, D), lambda j: (0, j))],
        out_specs=pl.BlockSpec((B, D), lambda j: (0, j)),
        out_shape=jax.ShapeDtypeStruct((B, N), F32),
        compiler_params=_params(("arbitrary",)),
        name="ada",
    )(c, w, b.reshape(1, N))


def _modulated_norm(x, g, sc, sh):
    ms = jnp.mean(x * x, axis=-1, keepdims=True)
    return (x * lax.rsqrt(ms + EPS) * g) * (1.0 + sc) + sh


def _inproj_kernel(x_ref, g_ref, sc_ref, sh_ref, w_ref, b_ref, wt_ref, bt_ref, z_ref, aux_ref, *, col_chunk):
    h = _modulated_norm(x_ref[0], g_ref[...], sc_ref[0], sh_ref[0])
    hb = h.astype(MXU_DT)
    n_cols = w_ref.shape[1]
    for c0 in range(0, n_cols, col_chunk):
        z_ref[0, :, c0:c0 + col_chunk] = _dot(hb, w_ref[:, c0:c0 + col_chunk]) + b_ref[:, c0:c0 + col_chunk]
    aux_ref[0] = _nt_dot(wt_ref[...], hb) + bt_ref[...]


def _inproj(x, g, sc, sh, w, b, wt, bt, tm):
    B, T, D = x.shape
    NZ = w.shape[1]
    NA = wt.shape[0]
    return pl.pallas_call(
        functools.partial(_inproj_kernel, col_chunk=256),
        grid=(B, T // tm),
        in_specs=[pl.BlockSpec((1, tm, D), lambda b_, i: (b_, i, 0)),
                  pl.BlockSpec((1, D), lambda b_, i: (0, 0)),
                  pl.BlockSpec((1, 1, D), lambda b_, i: (b_, 0, 0)),
                  pl.BlockSpec((1, 1, D), lambda b_, i: (b_, 0, 0)),
                  pl.BlockSpec((D, NZ), lambda b_, i: (0, 0)),
                  pl.BlockSpec((1, NZ), lambda b_, i: (0, 0)),
                  pl.BlockSpec((NA, D), lambda b_, i: (0, 0)),
                  pl.BlockSpec((NA, 1), lambda b_, i: (0, 0))],
        out_specs=[pl.BlockSpec((1, tm, NZ), lambda b_, i: (b_, i, 0)),
                   pl.BlockSpec((1, NA, tm), lambda b_, i: (b_, 0, i))],
        out_shape=[jax.ShapeDtypeStruct((B, T, NZ), F32),
                   jax.ShapeDtypeStruct((B, NA, T), F32)],
        compiler_params=_params(("parallel", "parallel")),
        name="inproj",
    )(x, g, sc, sh, w, b, wt, bt)


Z_MQ, Z_MK, Z_MV, Z_MO, Z_AQ, Z_IQ = 0, 512, 1024, 1536, 2048, 2560
Z_AK, Z_AV, Z_IK, Z_MI, Z_MF, Z_IW, Z_COLS = 3072, 3136, 3200, 3264, 3268, 3272, 3328
AUX_AV, AUX_MI, AUX_MF, AUX_IW, AUX_ROWS = 0, 64, 68, 72, 128


def _log_sigmoid(f):
    return jnp.minimum(f, 0.0) - jnp.log1p(jnp.exp(-jnp.abs(f)))


def _mlstm_kernel(qk_ref, vo_ref, gz_ref, aux_ref, cw_ref, gm_ref, o_ref, hist, c_s, n_s, m_s, *, rows, dh):
    i = pl.program_id(1)
    halo = SUBLANES
    L = CHUNK

    @pl.when(i == 0)
    def _():
        hist[0:halo, :] = jnp.zeros((halo, hist.shape[1]), F32)
        c_s[...] = jnp.zeros(c_s.shape, F32)
        n_s[...] = jnp.zeros(n_s.shape, F32)
        m_s[...] = jnp.zeros(m_s.shape, F32)

    hist[halo:halo + rows, :] = qk_ref[0]
    acc = hist[halo - 3:halo - 3 + rows, :] * cw_ref[0:1, :]
    for j in range(1, CONV_W):
        acc = acc + hist[halo - 3 + j:halo - 3 + j + rows, :] * cw_ref[j:j + 1, :]
    hist[0:halo, :] = hist[rows:rows + halo, :]
    qk = acc * (1.0 / (1.0 + jnp.exp(-acc)))
    width = M_HEADS * dh

    tt = lax.broadcasted_iota(I32, (L, L), 0)
    ss = lax.broadcasted_iota(I32, (L, L), 1)
    causal = ss <= tt

    for c in range(rows // L):
        r0 = c * L
        for h in range(M_HEADS):
            qh = qk[r0:r0 + L, h * dh:(h + 1) * dh] * (dh ** -0.5)
            kh = qk[r0:r0 + L, width + h * dh:width + (h + 1) * dh]
            vh = vo_ref[0, r0:r0 + L, h * dh:(h + 1) * dh]
            oh = vo_ref[0, r0:r0 + L, width + h * dh:width + (h + 1) * dh]
            gcol = Z_MI - Z_IK
            ig_col = gz_ref[0, r0:r0 + L, gcol + h:gcol + h + 1]
            fg_col = _log_sigmoid(gz_ref[0, r0:r0 + L, gcol + M_HEADS + h:gcol + M_HEADS + h + 1])
            ig_row = aux_ref[0, AUX_MI + h:AUX_MI + h + 1, r0:r0 + L]
            fg_row = _log_sigmoid(aux_ref[0, AUX_MF + h:AUX_MF + h + 1, r0:r0 + L])

            b_col = jnp.sum(jnp.where(causal, fg_row, 0.0), axis=1, keepdims=True)
            b_row = jnp.sum(jnp.where(tt <= ss, fg_col, 0.0), axis=0, keepdims=True)
            b_last = jnp.sum(fg_row, axis=1, keepdims=True)
            log_d = jnp.where(causal, b_col - b_row + ig_row, -jnp.inf)
            max_d = jnp.max(log_d, axis=1, keepdims=True)
            m_loc = jnp.max(b_last - b_row + ig_row, axis=1, keepdims=True)
            w_col = jnp.exp(b_last - b_col + ig_col - m_loc)
            kw = kh * w_col
            kv = _dot(kw.T.astype(MXU_DT), vh.astype(MXU_DT))
            ks = jnp.sum(kw, axis=0, keepdims=True)

            c0 = c_s[h]
            n0 = n_s[h:h + 1, :]
            m0 = m_s[h:h + 1, 0:1]
            m_inter = b_col + m0
            m_t = jnp.maximum(m_inter, max_d)
            w_inter = jnp.exp(m_inter - m_t)
            qb = qh.astype(MXU_DT)
            s = _nt_dot(qb, kh.astype(MXU_DT)) * jnp.exp(log_d - m_t)
            num = _dot(s.astype(MXU_DT), vh.astype(MXU_DT)) + w_inter * _dot(qb, c0.astype(MXU_DT))
            den = jnp.sum(s, axis=1, keepdims=True) + w_inter * jnp.sum(qh * n0, axis=1, keepdims=True)
            hh = num / jnp.maximum(jnp.abs(den), jnp.exp(-m_t))

            m_new = jnp.maximum(b_last + m0, m_loc)
            dec = jnp.exp(b_last + m0 - m_new)
            inj = jnp.exp(m_loc - m_new)
            c_s[h] = dec * c0 + inj * kv
            n_s[h:h + 1, :] = dec * n0 + inj * ks
            m_s[h:h + 1, :] = jnp.broadcast_to(m_new, (1, m_s.shape[1]))

            hn = hh * lax.rsqrt(jnp.mean(hh * hh, axis=-1, keepdims=True) + EPS) * gm_ref[:, h * dh:(h + 1) * dh]
            o_ref[0, r0:r0 + L, h * dh:(h + 1) * dh] = hn * (1.0 / (1.0 + jnp.exp(-oh)))


def _mlstm(z, aux, conv_w, gm, rows):
    B, T, _ = z.shape
    width = gm.shape[1]
    dh = width // M_HEADS
    kern = functools.partial(_mlstm_kernel, rows=rows, dh=dh)
    return pl.pallas_call(
        kern,
        grid=(B, T // rows),
        in_specs=[pl.BlockSpec((1, rows, 2 * width), lambda b_, i: (b_, i, 0)),
                  pl.BlockSpec((1, rows, 2 * width), lambda b_, i: (b_, i, 1)),
                  pl.BlockSpec((1, rows, LANES), lambda b_, i: (b_, i, Z_IK // LANES)),
                  pl.BlockSpec((1, AUX_ROWS, rows), lambda b_, i: (b_, 0, i)),
                  pl.BlockSpec((CONV_W, 2 * width), lambda b_, i: (0, 0)),
                  pl.BlockSpec((1, width), lambda b_, i: (0, 0))],
        out_specs=pl.BlockSpec((1, rows, width), lambda b_, i: (b_, i, 0)),
        out_shape=jax.ShapeDtypeStruct((B, T, width), F32),
        scratch_shapes=[pltpu.VMEM((rows + SUBLANES, 2 * width), F32),
                        pltpu.VMEM((M_HEADS, dh, dh), F32),
                        pltpu.VMEM((SUBLANES, dh), F32),
                        pltpu.VMEM((SUBLANES, LANES), F32)],
        compiler_params=_params(("parallel", "arbitrary")),
        name="mlstm",
    )(z, z, z, aux, conv_w, gm)


def _dsa_kernel(aq_ref, iq_ref, kv_ref, ik_ref, aux_ref, ga_ref, o_ref, key_s, thr_s, acc_s, tri_s, st_s, *, n_sel, dh):
    qb = pl.program_id(1)
    QB = Q_BLOCK
    KB = 2 * QB
    nkb = (qb + 2) // 2
    q0 = pl.multiple_of(qb * QB, QB)
    idx_scale = (IDX_DH ** -0.5) * (IDX_HEADS ** -0.5)
    row = lax.broadcasted_iota(I32, (KB, QB), 0)
    col = lax.broadcasted_iota(I32, (KB, QB), 1)

    iqb = iq_ref[0].astype(MXU_DT)
    w_rows = aux_ref[0, AUX_IW:AUX_IW + IDX_HEADS, pl.ds(q0, QB)] * idx_scale

    def score_body(j, carry):
        k0 = pl.multiple_of(j * KB, KB)
        ki = ik_ref[0, pl.ds(k0, KB), 0:IDX_DH].astype(MXU_DT)
        sc = jnp.zeros((KB, QB), F32)
        for h in range(IDX_HEADS):
            r = _nt_dot(ki, iqb[:, h * IDX_DH:(h + 1) * IDX_DH])
            sc = sc + jnp.maximum(r, 0.0) * w_rows[h:h + 1, :]
        key_s[pl.ds(k0, KB), :] = jnp.where(k0 + row <= q0 + col, sc, -jnp.inf)
        return carry

    lax.fori_loop(0, nkb, score_body, 0)

    def count(cand, strict):
        def body(j, c8):
            k = key_s[pl.ds(pl.multiple_of(j * KB, KB), KB), :]
            hit = (k > cand) if strict else (k >= cand)
            return c8 + jnp.sum(jnp.where(hit, 1, 0).astype(I32).reshape(KB // SUBLANES, SUBLANES, QB), axis=0)
        c8 = lax.fori_loop(0, nkb, body, jnp.zeros((SUBLANES, QB), I32))
        return jnp.sum(c8, axis=0, keepdims=True)

    def key_to_float(key):
        return lax.bitcast_convert_type(jnp.where(key < 0, key ^ jnp.int32(0x7FFFFFFF), key), F32)

    thr_s[0:1, :] = jnp.full((1, QB), -jnp.inf, F32)
    thr_s[1:2, :] = jnp.zeros((1, QB), F32)

    @pl.when(qb >= n_sel // QB)
    def _():
        c_pos = count(jnp.zeros((1, QB), F32), False)
        prefix = jnp.where(c_pos >= n_sel, 0, jnp.int32(INT_MIN)).astype(I32)

        def bit_body(it, prefix):
            cand = prefix | jnp.left_shift(jnp.int32(1), 30 - it)
            c = count(key_to_float(cand), False)
            return jnp.where(c >= n_sel, cand, prefix)

        thr_f = key_to_float(lax.fori_loop(0, 31, bit_body, prefix))
        thr_s[0:1, :] = thr_f
        thr_s[1:2, :] = (n_sel - count(thr_f, True)).astype(F32)

    thr = thr_s[0:1, :]
    need = thr_s[1:2, :]

    aqb = (aq_ref[0] * (dh ** -0.5)).astype(MXU_DT)
    acc_s[...] = jnp.zeros(acc_s.shape, F32)
    krow = lax.broadcasted_iota(I32, (KB, KB), 0)
    kcol = lax.broadcasted_iota(I32, (KB, KB), 1)
    tri_s[...] = jnp.where(kcol < krow, 1.0, 0.0).astype(MXU_DT)

    def att_body(j, carry):
        rc, ms, ls = carry
        k0 = pl.multiple_of(j * KB, KB)
        kb = kv_ref[0, pl.ds(k0, KB), 0:dh].astype(MXU_DT)
        vt = aux_ref[0, AUX_AV:AUX_AV + dh, pl.ds(k0, KB)].astype(MXU_DT)
        key = key_s[pl.ds(k0, KB), :]
        tie = key == thr
        tie_f = jnp.where(tie, 1.0, 0.0)
        before = _dot(tri_s[...], tie_f.astype(MXU_DT)) + rc
        sel = (key > thr) | (tie & (before < need))
        bias = jnp.where(sel, 0.0, -jnp.inf)
        new_ms, new_ls = [], []
        for h in range(A_HEADS):
            st = _nt_dot(kb, aqb[:, h * dh:(h + 1) * dh]) + bias
            st_s[h] = st
            new_ms.append(jnp.maximum(ms[h], jnp.max(st, axis=0, keepdims=True)))
        for h in range(A_HEADS):
            m_new = new_ms[h]
            p = jnp.exp(st_s[h] - m_new)
            alpha = jnp.exp(ms[h] - m_new)
            new_ls.append(alpha * ls[h] + jnp.sum(p, axis=0, keepdims=True))
            acc_s[:, h * QB:(h + 1) * QB] = alpha * acc_s[:, h * QB:(h + 1) * QB] + _dot(vt, p.astype(MXU_DT))
        return rc + jnp.sum(tie_f, axis=0, keepdims=True), tuple(new_ms), tuple(new_ls)

    init = (jnp.zeros((1, QB), F32),
            tuple(jnp.full((1, QB), NEG_BIG, F32) for _ in range(A_HEADS)),
            tuple(jnp.zeros((1, QB), F32) for _ in range(A_HEADS)))
    _, _, ls = lax.fori_loop(0, nkb, att_body, init)

    for h in range(A_HEADS):
        o = acc_s[:, h * QB:(h + 1) * QB] / ls[h]
        o = o * lax.rsqrt(jnp.mean(o * o, axis=0, keepdims=True) + EPS)
        o_ref[0, :, h * dh:(h + 1) * dh] = o.T * ga_ref[:, h * dh:(h + 1) * dh]


def _dsa(z, aux, ga, n_sel):
    B, T, _ = z.shape
    width = ga.shape[1]
    dh = width // A_HEADS
    assert n_sel % Q_BLOCK == 0 and T % (2 * Q_BLOCK) == 0 and dh == Z_AV - Z_AK
    kern = functools.partial(_dsa_kernel, n_sel=n_sel, dh=dh)
    return pl.pallas_call(
        kern,
        grid=(B, T // Q_BLOCK),
        in_specs=[pl.BlockSpec((1, Q_BLOCK, width), lambda b_, i: (b_, i, Z_AQ // width)),
                  pl.BlockSpec((1, Q_BLOCK, width), lambda b_, i: (b_, i, Z_IQ // width)),
                  pl.BlockSpec((1, T, LANES), lambda b_, i: (b_, 0, Z_AK // LANES)),
                  pl.BlockSpec((1, T, LANES), lambda b_, i: (b_, 0, Z_IK // LANES)),
                  pl.BlockSpec((1, AUX_ROWS, T), lambda b_, i: (b_, 0, 0)),
                  pl.BlockSpec((1, width), lambda b_, i: (0, 0))],
        out_specs=pl.BlockSpec((1, Q_BLOCK, width), lambda b_, i: (b_, i, 0)),
        out_shape=jax.ShapeDtypeStruct((B, T, width), F32),
        scratch_shapes=[pltpu.VMEM((T, Q_BLOCK), F32),
                        pltpu.VMEM((SUBLANES, Q_BLOCK), F32),
                        pltpu.VMEM((dh, A_HEADS * Q_BLOCK), F32),
                        pltpu.VMEM((2 * Q_BLOCK, 2 * Q_BLOCK), MXU_DT),
                        pltpu.VMEM((A_HEADS, 2 * Q_BLOCK, Q_BLOCK), F32)],
        compiler_params=_params(("parallel", "arbitrary")),
        name="dsa",
    )(z, z, z, z, aux, ga)


def _post_kernel(hm_ref, ha_ref, x_ref, wo_ref, g1_ref, g_ref, sc_ref, sh_ref, wq_ref, sk_ref,
                 x1_ref, h2t_ref, st_ref):
    wm = hm_ref.shape[2]
    y = _dot(hm_ref[0].astype(MXU_DT), wo_ref[0:wm, :]) + _dot(ha_ref[0].astype(MXU_DT), wo_ref[wm:, :])
    x1 = x_ref[0] + g1_ref[0] * y
    x1_ref[0] = x1
    h2 = _modulated_norm(x1, g_ref[...], sc_ref[0], sh_ref[0])
    h2t_ref[...] = h2.T.astype(h2t_ref.dtype)
    qry = _dot(h2.astype(MXU_DT), wq_ref[...]).astype(MXU_DT)
    dk = sk_ref.shape[2]
    for hp in range(sk_ref.shape[0]):
        st_ref[hp] = _nt_dot(sk_ref[hp], qry[:, hp * dk:(hp + 1) * dk])


def _post(hm, ha, x, wo, g1, g, sc, sh, wq, sk, tm):
    B, T, D = x.shape
    nt = T // tm
    wm = hm.shape[2]
    NS, NK, DK = sk.shape
    return pl.pallas_call(
        _post_kernel,
        grid=(B, nt),
        in_specs=[pl.BlockSpec((1, tm, wm), lambda b_, i: (b_, i, 0)),
                  pl.BlockSpec((1, tm, ha.shape[2]), lambda b_, i: (b_, i, 0)),
                  pl.BlockSpec((1, tm, D), lambda b_, i: (b_, i, 0)),
                  pl.BlockSpec(wo.shape, lambda b_, i: (0, 0)),
                  pl.BlockSpec((1, 1, D), lambda b_, i: (b_, 0, 0)),
                  pl.BlockSpec((1, D), lambda b_, i: (0, 0)),
                  pl.BlockSpec((1, 1, D), lambda b_, i: (b_, 0, 0)),
                  pl.BlockSpec((1, 1, D), lambda b_, i: (b_, 0, 0)),
                  pl.BlockSpec(wq.shape, lambda b_, i: (0, 0)),
                  pl.BlockSpec(sk.shape, lambda b_, i: (0, 0, 0))],
        out_specs=[pl.BlockSpec((1, tm, D), lambda b_, i: (b_, i, 0)),
                   pl.BlockSpec((D, tm), lambda b_, i: (0, b_ * nt + i)),
                   pl.BlockSpec((NS, NK, tm), lambda b_, i: (0, 0, b_ * nt + i))],
        out_shape=[jax.ShapeDtypeStruct((B, T, D), F32),
                   jax.ShapeDtypeStruct((D, B * T), MXU_DT),
                   jax.ShapeDtypeStruct((NS, NK, B * T), F32)],
        compiler_params=_params(("parallel", "parallel")),
        name="post",
    )(hm, ha, x, wo, g1, g, sc, sh, wq, sk)


def _top_sorted(cur, n):
    R = cur.shape[0]
    rows = lax.broadcasted_iota(I32, cur.shape, 0)
    vals = []
    for k in range(n):
        mx = jnp.max(cur, axis=0, keepdims=True)
        vals.append(mx)
        if k + 1 < n:
            first = jnp.min(jnp.where(cur == mx, rows, R), axis=0, keepdims=True)
            cur = jnp.where(rows == first, -jnp.inf, cur)
    return vals


def _select_kernel(st_ref, d1_ref, e1_ref, e2_ref, cand_s):
    n_top = P_TOPK + 1
    pairs = [(a, b) for a in range(n_top) for b in range(n_top) if (a + 1) * (b + 1) <= n_top]
    for h in range(P_HEADS):
        s1 = st_ref[2 * h]
        s2 = st_ref[2 * h + 1]
        v1 = _top_sorted(s1, n_top)
        v2 = _top_sorted(s2, n_top)
        cand_s[...] = jnp.full(cand_s.shape, -jnp.inf, F32)
        for r, (a, b) in enumerate(pairs):
            cand_s[r:r + 1, :] = v1[a] + v2[b]
        top = _top_sorted(cand_s[...], n_top)
        thr = 0.5 * (top[P_TOPK - 1] + top[P_TOPK])
        zsum = jnp.zeros_like(thr)
        for k in range(P_TOPK):
            zsum = zsum + jnp.exp(top[k] - top[0])
        d1_ref[h] = jnp.exp(thr - s1 - v2[0])
        e1_ref[h] = jnp.exp(s1 - v1[0]) / zsum
        e2_ref[h] = jnp.exp(s2 - v2[0])


def _select(st, tl):
    NS, NK, N = st.shape
    n_top = P_TOPK + 1
    n_pairs = sum(1 for a in range(n_top) for b in range(n_top) if (a + 1) * (b + 1) <= n_top)
    cand_rows = -(-n_pairs // SUBLANES) * SUBLANES
    out = jax.ShapeDtypeStruct((P_HEADS, NK, N), F32)
    spec = pl.BlockSpec((P_HEADS, NK, tl), lambda i: (0, 0, i))
    return pl.pallas_call(
        _select_kernel,
        grid=(N // tl,),
        in_specs=[pl.BlockSpec((NS, NK, tl), lambda i: (0, 0, i))],
        out_specs=[spec, spec, spec],
        out_shape=[out, out, out],
        scratch_shapes=[pltpu.VMEM((cand_rows, tl), F32)],
        compiler_params=_params(("parallel",)),
        name="select",
    )(st)


def _gelu(a):
    return 0.5 * a * (1.0 + lax.erf(a * (2.0 ** -0.5)))


def _expert_kernel(h2t_ref, u_ref, vt_ref, d1_ref, e1_ref, e2_ref, o_ref, acc_s, p_s, *, lane_tile):
    e = pl.program_id(1)
    te = u_ref.shape[0]
    tn = h2t_ref.shape[1]
    rows_per_step = te // N_KEYS

    @pl.when(e == 0)
    def _():
        acc_s[...] = jnp.zeros(acc_s.shape, F32)

    a_t = _dot(u_ref[...], h2t_ref[...])
    for ii in range(rows_per_step):
        i = e * rows_per_step + ii
        for l0 in range(0, tn, lane_tile):
            d_rows = [d1_ref[h, pl.ds(i, 1), l0:l0 + lane_tile] for h in range(P_HEADS)]
            e_rows = [e1_ref[h, pl.ds(i, 1), l0:l0 + lane_tile] for h in range(P_HEADS)]
            for c0 in range(0, lane_tile, LANES):
                g = jnp.zeros((N_KEYS, LANES), F32)
                for h in range(P_HEADS):
                    e2 = e2_ref[h, :, l0 + c0:l0 + c0 + LANES]
                    g = g + jnp.where(e2 > d_rows[h][:, c0:c0 + LANES], e2 * e_rows[h][:, c0:c0 + LANES], 0.0)
                a_blk = a_t[ii * N_KEYS:(ii + 1) * N_KEYS, l0 + c0:l0 + c0 + LANES]
                p_s[ii * N_KEYS:(ii + 1) * N_KEYS, l0 + c0:l0 + c0 + LANES] = (g * _gelu(a_blk)).astype(p_s.dtype)
    acc_s[...] += _dot(vt_ref[...], p_s[...])

    @pl.when(e == pl.num_programs(1) - 1)
    def _():
        o_ref[...] = acc_s[...]


def _expert(h2t, u, vt, d1, e1, e2, tn, te):
    D, N = h2t.shape
    NE = u.shape[0]
    NK = d1.shape[1]
    kern = functools.partial(_expert_kernel, lane_tile=min(tn, 2 * LANES))
    hspec = pl.BlockSpec((P_HEADS, NK, tn), lambda t, e: (0, 0, t))
    return pl.pallas_call(
        kern,
        grid=(N // tn, NE // te),
        in_specs=[pl.BlockSpec((D, tn), lambda t, e: (0, t)),
                  pl.BlockSpec((te, D), lambda t, e: (e, 0)),
                  pl.BlockSpec((D, te), lambda t, e: (0, e)),
                  hspec, hspec, hspec],
        out_specs=pl.BlockSpec((D, tn), lambda t, e: (0, t)),
        out_shape=jax.ShapeDtypeStruct((D, N), F32),
        scratch_shapes=[pltpu.VMEM((D, tn), F32),
                        pltpu.VMEM((te, tn), MXU_DT)],
        compiler_params=_params(("parallel", "arbitrary")),
        name="expert",
    )(h2t, u, vt, d1, e1, e2)


def _final_kernel(x1_ref, pt_ref, g2_ref, gf_ref, o_ref, *, normalize):
    x2 = x1_ref[0] + g2_ref[0] * pt_ref[...].T
    if normalize:
        ms = jnp.mean(x2 * x2, axis=-1, keepdims=True)
        x2 = x2 * lax.rsqrt(ms + EPS) * gf_ref[...]
    o_ref[0] = x2


def _final(x1, pt, g2, gf, tm, normalize):
    B, T, D = x1.shape
    nt = T // tm
    return pl.pallas_call(
        functools.partial(_final_kernel, normalize=normalize),
        grid=(B, nt),
        in_specs=[pl.BlockSpec((1, tm, D), lambda b_, i: (b_, i, 0)),
                  pl.BlockSpec((D, tm), lambda b_, i: (0, b_ * nt + i)),
                  pl.BlockSpec((1, 1, D), lambda b_, i: (b_, 0, 0)),
                  pl.BlockSpec((1, D), lambda b_, i: (0, 0))],
        out_specs=pl.BlockSpec((1, tm, D), lambda b_, i: (b_, i, 0)),
        out_shape=jax.ShapeDtypeStruct((B, T, D), F32),
        compiler_params=_params(("parallel", "parallel")),
        name="final",
    )(x1, pt, g2, gf)


def _pack_in_weights(w_in, b_in):
    D = w_in.shape[0]
    o_mi, o_aq, o_ak, o_av, o_iq, o_ik, o_iw, o_end = 2048, 2056, 2568, 2632, 2696, 3208, 3272, 3280
    order = [(0, o_mi), (o_aq, o_ak), (o_iq, o_ik), (o_ak, o_av), (o_av, o_iq), (o_ik, o_iw),
             (o_mi, o_aq), (o_iw, o_end)]
    n_used = sum(hi - lo for lo, hi in order)
    w = jnp.concatenate([w_in[:, lo:hi] for lo, hi in order] + [jnp.zeros((D, Z_COLS - n_used), F32)], axis=1)
    b = jnp.concatenate([b_in[lo:hi] for lo, hi in order] + [jnp.zeros((Z_COLS - n_used,), F32)])
    aux_order = [(o_av, o_iq), (o_mi, o_aq), (o_iw, o_end)]
    n_aux = sum(hi - lo for lo, hi in aux_order)
    wt = jnp.concatenate([w_in[:, lo:hi] for lo, hi in aux_order] + [jnp.zeros((D, AUX_ROWS - n_aux), F32)], axis=1).T
    bt = jnp.concatenate([b_in[lo:hi] for lo, hi in aux_order] + [jnp.zeros((AUX_ROWS - n_aux,), F32)])
    return w.astype(MXU_DT), b.reshape(1, Z_COLS), wt.astype(MXU_DT), bt.reshape(AUX_ROWS, 1)


def _layer(x, ada, norm1_g, w_in, b_in, conv_qk, out_norm_m, out_norm_a, w_out, norm2_g,
           w_peer_q, peer_sub_keys, peer_u, peer_v, *, tm, mrows, tl, tn, te):
    B, T, D = x.shape
    sh1, sc1, g1, sh2, sc2, g2 = [a.reshape(B, 1, D) for a in jnp.split(ada, 6, axis=-1)]
    w, b, wt, bt = _pack_in_weights(w_in, b_in)
    z, aux = _inproj(x, norm1_g.reshape(1, D), sc1, sh1, w, b, wt, bt, tm)
    hm = _mlstm(z, aux, conv_qk, out_norm_m.reshape(1, -1), mrows)
    ha = _dsa(z, aux, out_norm_a.reshape(1, -1), min(TOPK_MAX, T // 4))
    sk = peer_sub_keys.reshape(2 * P_HEADS, N_KEYS, -1).astype(MXU_DT)
    x1, h2t, st = _post(hm, ha, x, w_out.astype(MXU_DT), g1, norm2_g.reshape(1, D), sc2, sh2,
                        w_peer_q.astype(MXU_DT), sk, tm)
    d1, e1, e2 = _select(st, tl)
    pt = _expert(h2t, peer_u.astype(MXU_DT), peer_v.T.astype(MXU_DT), d1, e1, e2, tn, te)
    return x1, pt, g2


def kernel(x, c, w_ada, b_ada, norm1_g, w_in, b_in, conv_qk, out_norm_m, out_norm_a, w_out, norm2_g,
           w_peer_q, peer_sub_keys, peer_u, peer_v, final_g):
    B, T, D = x.shape
    depth = w_ada.shape[0]
    tm = min(512, T)
    for l in range(depth):
        ada = _ada(c, w_ada[l], b_ada[l])
        x1, pt, g2 = _layer(x, ada, norm1_g[l], w_in[l], b_in[l], conv_qk[l], out_norm_m[l], out_norm_a[l],
                            w_out[l], norm2_g[l], w_peer_q[l], peer_sub_keys[l], peer_u[l], peer_v[l],
                            tm=tm, mrows=min(256, T), tl=256, tn=512, te=512)
        x = _final(x1, pt, g2, final_g.reshape(1, D), tm, normalize=(l + 1 == depth))
    return x
```

```python
import functools

import jax
import jax.numpy as jnp
from jax import lax
from jax.experimental import pallas as pl
from jax.experimental.pallas import tpu as pltpu

F32 = jnp.float32
I32 = jnp.int32
MXU_DT = jnp.bfloat16
EPS = 1e-6

M_HEADS = 4
CONV_W = 4
CHUNK = 64
A_HEADS = 8
IDX_HEADS = 8
IDX_DH = 64
TOPK_MAX = 256
Q_BLOCK = 128
P_HEADS = 8
N_KEYS = 128
P_TOPK = 16

LANES = 128
SUBLANES = 8
VMEM_LIMIT = 56 * 1024 * 1024

INT_MIN = -(2 ** 31)
NEG_BIG = -1e30


def _params(sem, vmem=VMEM_LIMIT):
    return pltpu.CompilerParams(dimension_semantics=sem, vmem_limit_bytes=vmem)


def _nt_dot(a, b):
    return lax.dot_general(a, b, (((1,), (1,)), ((), ())), preferred_element_type=F32)


def _dot(a, b):
    return jnp.dot(a, b, preferred_element_type=F32)


def _ada_kernel(c_ref, w_ref, b_ref, o_ref):
    c = c_ref[...]
    sc = c * (1.0 / (1.0 + jnp.exp(-c)))
    o_ref[...] = jnp.dot(sc, w_ref[...], preferred_element_type=F32,
                         precision=lax.Precision.HIGHEST) + b_ref[...]


def _ada(c, w, b):
    B, D = c.shape
    N = w.shape[1]
    return pl.pallas_call(
        _ada_kernel,
        grid=(N // D,),
        in_specs=[pl.BlockSpec((B, D), lambda j: (0, 0)),
                  pl.BlockSpec((D, D), lambda j: (0, j)),
                  pl.BlockSpec((1, D), lambda j: (0, j))],
        out_specs=pl.BlockSpec((B, D), lambda j: (0, j)),
        out_shape=jax.ShapeDtypeStruct((B, N), F32),
        compiler_params=_params(("arbitrary",)),
        name="ada",
    )(c, w, b.reshape(1, N))


def _modulated_norm(x, g, sc, sh):
    ms = jnp.mean(x * x, axis=-1, keepdims=True)
    return (x * lax.rsqrt(ms + EPS) * g) * (1.0 + sc) + sh


def _inproj_kernel(x_ref, g_ref, sc_ref, sh_ref, w_ref, b_ref, wt_ref, bt_ref, z_ref, aux_ref, *, col_chunk):
    h = _modulated_norm(x_ref[0], g_ref[...], sc_ref[0], sh_ref[0])
    hb = h.astype(MXU_DT)
    n_cols = w_ref.shape[1]
    for c0 in range(0, n_cols, col_chunk):
        z_ref[0, :, c0:c0 + col_chunk] = _dot(hb, w_ref[:, c0:c0 + col_chunk]) + b_ref[:, c0:c0 + col_chunk]
    aux_ref[0] = _nt_dot(wt_ref[...], hb) + bt_ref[...]


def _inproj(x, g, sc, sh, w, b, wt, bt, tm):
    B, T, D = x.shape
    NZ = w.shape[1]
    NA = wt.shape[0]
    return pl.pallas_call(
        functools.partial(_inproj_kernel, col_chunk=256),
        grid=(B, T // tm),
        in_specs=[pl.BlockSpec((1, tm, D), lambda b_, i: (b_, i, 0)),
                  pl.BlockSpec((1, D), lambda b_, i: (0, 0)),
                  pl.BlockSpec((1, 1, D), lambda b_, i: (b_, 0, 0)),
                  pl.BlockSpec((1, 1, D), lambda b_, i: (b_, 0, 0)),
                  pl.BlockSpec((D, NZ), lambda b_, i: (0, 0)),
                  pl.BlockSpec((1, NZ), lambda b_, i: (0, 0)),
                  pl.BlockSpec((NA, D), lambda b_, i: (0, 0)),
                  pl.BlockSpec((NA, 1), lambda b_, i: (0, 0))],
        out_specs=[pl.BlockSpec((1, tm, NZ), lambda b_, i: (b_, i, 0)),
                   pl.BlockSpec((1, NA, tm), lambda b_, i: (b_, 0, i))],
        out_shape=[jax.ShapeDtypeStruct((B, T, NZ), F32),
                   jax.ShapeDtypeStruct((B, NA, T), F32)],
        compiler_params=_params(("parallel", "parallel")),
        name="inproj",
    )(x, g, sc, sh, w, b, wt, bt)


Z_MQ, Z_MK, Z_MV, Z_MO, Z_AQ, Z_IQ = 0, 512, 1024, 1536, 2048, 2560
Z_AK, Z_AV, Z_IK, Z_MI, Z_MF, Z_IW, Z_COLS = 3072, 3136, 3200, 3264, 3268, 3272, 3328
AUX_AV, AUX_MI, AUX_MF, AUX_IW, AUX_ROWS = 0, 64, 68, 72, 128


def _log_sigmoid(f):
    return jnp.minimum(f, 0.0) - jnp.log1p(jnp.exp(-jnp.abs(f)))


def _mlstm_kernel(qk_ref, vo_ref, gz_ref, aux_ref, cw_ref, gm_ref, o_ref, hist, c_s, n_s, m_s, *, rows, dh):
    i = pl.program_id(1)
    halo = SUBLANES
    L = CHUNK

    @pl.when(i == 0)
    def _():
        hist[0:halo, :] = jnp.zeros((halo, hist.shape[1]), F32)
        c_s[...] = jnp.zeros(c_s.shape, F32)
        n_s[...] = jnp.zeros(n_s.shape, F32)
        m_s[...] = jnp.zeros(m_s.shape, F32)

    hist[halo:halo + rows, :] = qk_ref[0]
    acc = hist[halo - 3:halo - 3 + rows, :] * cw_ref[0:1, :]
    for j in range(1, CONV_W):
        acc = acc + hist[halo - 3 + j:halo - 3 + j + rows, :] * cw_ref[j:j + 1, :]
    hist[0:halo, :] = hist[rows:rows + halo, :]
    qk = acc * (1.0 / (1.0 + jnp.exp(-acc)))
    width = M_HEADS * dh

    tt = lax.broadcasted_iota(I32, (L, L), 0)
    ss = lax.broadcasted_iota(I32, (L, L), 1)
    causal = ss <= tt

    for c in range(rows // L):
        r0 = c * L
        for h in range(M_HEADS):
            qh = qk[r0:r0 + L, h * dh:(h + 1) * dh] * (dh ** -0.5)
            kh = qk[r0:r0 + L, width + h * dh:width + (h + 1) * dh]
            vh = vo_ref[0, r0:r0 + L, h * dh:(h + 1) * dh]
            oh = vo_ref[0, r0:r0 + L, width + h * dh:width + (h + 1) * dh]
            gcol = Z_MI - Z_IK
            ig_col = gz_ref[0, r0:r0 + L, gcol + h:gcol + h + 1]
            fg_col = _log_sigmoid(gz_ref[0, r0:r0 + L, gcol + M_HEADS + h:gcol + M_HEADS + h + 1])
            ig_row = aux_ref[0, AUX_MI + h:AUX_MI + h + 1, r0:r0 + L]
            fg_row = _log_sigmoid(aux_ref[0, AUX_MF + h:AUX_MF + h + 1, r0:r0 + L])

            b_col = jnp.sum(jnp.where(causal, fg_row, 0.0), axis=1, keepdims=True)
            b_row = jnp.sum(jnp.where(tt <= ss, fg_col, 0.0), axis=0, keepdims=True)
            b_last = jnp.sum(fg_row, axis=1, keepdims=True)
            log_d = jnp.where(causal, b_col - b_row + ig_row, -jnp.inf)
            max_d = jnp.max(log_d, axis=1, keepdims=True)
            m_loc = jnp.max(b_last - b_row + ig_row, axis=1, keepdims=True)
            w_col = jnp.exp(b_last - b_col + ig_col - m_loc)
            kw = kh * w_col
            kv = _dot(kw.T.astype(MXU_DT), vh.astype(MXU_DT))
            ks = jnp.sum(kw, axis=0, keepdims=True)

            c0 = c_s[h]
            n0 = n_s[h:h + 1, :]
            m0 = m_s[h:h + 1, 0:1]
            m_inter = b_col + m0
            m_t = jnp.maximum(m_inter, max_d)
            w_inter = jnp.exp(m_inter - m_t)
            qb = qh.astype(MXU_DT)
            s = _nt_dot(qb, kh.astype(MXU_DT)) * jnp.exp(log_d - m_t)
            num = _dot(s.astype(MXU_DT), vh.astype(MXU_DT)) + w_inter * _dot(qb, c0.astype(MXU_DT))
            den = jnp.sum(s, axis=1, keepdims=True) + w_inter * jnp.sum(qh * n0, axis=1, keepdims=True)
            hh = num / jnp.maximum(jnp.abs(den), jnp.exp(-m_t))

            m_new = jnp.maximum(b_last + m0, m_loc)
            dec = jnp.exp(b_last + m0 - m_new)
            inj = jnp.exp(m_loc - m_new)
            c_s[h] = dec * c0 + inj * kv
            n_s[h:h + 1, :] = dec * n0 + inj * ks
            m_s[h:h + 1, :] = jnp.broadcast_to(m_new, (1, m_s.shape[1]))

            hn = hh * lax.rsqrt(jnp.mean(hh * hh, axis=-1, keepdims=True) + EPS) * gm_ref[:, h * dh:(h + 1) * dh]
            o_ref[0, r0:r0 + L, h * dh:(h + 1) * dh] = hn * (1.0 / (1.0 + jnp.exp(-oh)))


def _mlstm(z, aux, conv_w, gm, rows):
    B, T, _ = z.shape
    width = gm.shape[1]
    dh = width // M_HEADS
    kern = functools.partial(_mlstm_kernel, rows=rows, dh=dh)
    return pl.pallas_call(
        kern,
        grid=(B, T // rows),
        in_specs=[pl.BlockSpec((1, rows, 2 * width), lambda b_, i: (b_, i, 0)),
                  pl.BlockSpec((1, rows, 2 * width), lambda b_, i: (b_, i, 1)),
                  pl.BlockSpec((1, rows, LANES), lambda b_, i: (b_, i, Z_IK // LANES)),
                  pl.BlockSpec((1, AUX_ROWS, rows), lambda b_, i: (b_, 0, i)),
                  pl.BlockSpec((CONV_W, 2 * width), lambda b_, i: (0, 0)),
                  pl.BlockSpec((1, width), lambda b_, i: (0, 0))],
        out_specs=pl.BlockSpec((1, rows, width), lambda b_, i: (b_, i, 0)),
        out_shape=jax.ShapeDtypeStruct((B, T, width), F32),
        scratch_shapes=[pltpu.VMEM((rows + SUBLANES, 2 * width), F32),
                        pltpu.VMEM((M_HEADS, dh, dh), F32),
                        pltpu.VMEM((SUBLANES, dh), F32),
                        pltpu.VMEM((SUBLANES, LANES), F32)],
        compiler_params=_params(("parallel", "arbitrary")),
        name="mlstm",
    )(z, z, z, aux, conv_w, gm)


def _dsa_kernel(aq_ref, iq_ref, kv_ref, ik_ref, aux_ref, ga_ref, o_ref, key_s, thr_s, acc_s, tri_s, st_s, *, n_sel, dh):
    qb = pl.program_id(1)
    QB = Q_BLOCK
    KB = 2 * QB
    nkb = (qb + 2) // 2
    q0 = pl.multiple_of(qb * QB, QB)
    idx_scale = (IDX_DH ** -0.5) * (IDX_HEADS ** -0.5)
    row = lax.broadcasted_iota(I32, (KB, QB), 0)
    col = lax.broadcasted_iota(I32, (KB, QB), 1)

    iqb = iq_ref[0].astype(MXU_DT)
    w_rows = aux_ref[0, AUX_IW:AUX_IW + IDX_HEADS, pl.ds(q0, QB)] * idx_scale

    def score_body(j, carry):
        k0 = pl.multiple_of(j * KB, KB)
        ki = ik_ref[0, pl.ds(k0, KB), 0:IDX_DH].astype(MXU_DT)
        sc = jnp.zeros((KB, QB), F32)
        for h in range(IDX_HEADS):
            r = _nt_dot(ki, iqb[:, h * IDX_DH:(h + 1) * IDX_DH])
            sc = sc + jnp.maximum(r, 0.0) * w_rows[h:h + 1, :]
        key_s[pl.ds(k0, KB), :] = jnp.where(k0 + row <= q0 + col, sc, -jnp.inf)
        return carry

    lax.fori_loop(0, nkb, score_body, 0)

    def count(cand, strict):
        def body(j, c8):
            k = key_s[pl.ds(pl.multiple_of(j * KB, KB), KB), :]
            hit = (k > cand) if strict else (k >= cand)
            return c8 + jnp.sum(jnp.where(hit, 1, 0).astype(I32).reshape(KB // SUBLANES, SUBLANES, QB), axis=0)
        c8 = lax.fori_loop(0, nkb, body, jnp.zeros((SUBLANES, QB), I32))
        return jnp.sum(c8, axis=0, keepdims=True)

    def key_to_float(key):
        return lax.bitcast_convert_type(jnp.where(key < 0, key ^ jnp.int32(0x7FFFFFFF), key), F32)

    thr_s[0:1, :] = jnp.full((1, QB), -jnp.inf, F32)
    thr_s[1:2, :] = jnp.zeros((1, QB), F32)

    @pl.when(qb >= n_sel // QB)
    def _():
        c_pos = count(jnp.zeros((1, QB), F32), False)
        prefix = jnp.where(c_pos >= n_sel, 0, jnp.int32(INT_MIN)).astype(I32)

        def bit_body(it, prefix):
            cand = prefix | jnp.left_shift(jnp.int32(1), 30 - it)
            c = count(key_to_float(cand), False)
            return jnp.where(c >= n_sel, cand, prefix)

        thr_f = key_to_float(lax.fori_loop(0, 31, bit_body, prefix))
        thr_s[0:1, :] = thr_f
        thr_s[1:2, :] = (n_sel - count(thr_f, True)).astype(F32)

    thr = thr_s[0:1, :]
    need = thr_s[1:2, :]

    aqb = (aq_ref[0] * (dh ** -0.5)).astype(MXU_DT)
    acc_s[...] = jnp.zeros(acc_s.shape, F32)
    krow = lax.broadcasted_iota(I32, (KB, KB), 0)
    kcol = lax.broadcasted_iota(I32, (KB, KB), 1)
    tri_s[...] = jnp.where(kcol < krow, 1.0, 0.0).astype(MXU_DT)

    def att_body(j, carry):
        rc, ms, ls = carry
        k0 = pl.multiple_of(j * KB, KB)
        kb = kv_ref[0, pl.ds(k0, KB), 0:dh].astype(MXU_DT)
        vt = aux_ref[0, AUX_AV:AUX_AV + dh, pl.ds(k0, KB)].astype(MXU_DT)
        key = key_s[pl.ds(k0, KB), :]
        tie = key == thr
        tie_f = jnp.where(tie, 1.0, 0.0)
        before = _dot(tri_s[...], tie_f.astype(MXU_DT)) + rc
        sel = (key > thr) | (tie & (before < need))
        bias = jnp.where(sel, 0.0, -jnp.inf)
        new_ms, new_ls = [], []
        for h in range(A_HEADS):
            st = _nt_dot(kb, aqb[:, h * dh:(h + 1) * dh]) + bias
            st_s[h] = st
            new_ms.append(jnp.maximum(ms[h], jnp.max(st, axis=0, keepdims=True)))
        for h in range(A_HEADS):
            m_new = new_ms[h]
            p = jnp.exp(st_s[h] - m_new)
            alpha = jnp.exp(ms[h] - m_new)
            new_ls.append(alpha * ls[h] + jnp.sum(p, axis=0, keepdims=True))
            acc_s[:, h * QB:(h + 1) * QB] = alpha * acc_s[:, h * QB:(h + 1) * QB] + _dot(vt, p.astype(MXU_DT))
        return rc + jnp.sum(tie_f, axis=0, keepdims=True), tuple(new_ms), tuple(new_ls)

    init = (jnp.zeros((1, QB), F32),
            tuple(jnp.full((1, QB), NEG_BIG, F32) for _ in range(A_HEADS)),
            tuple(jnp.zeros((1, QB), F32) for _ in range(A_HEADS)))
    _, _, ls = lax.fori_loop(0, nkb, att_body, init)

    for h in range(A_HEADS):
        o = acc_s[:, h * QB:(h + 1) * QB] / ls[h]
        o = o * lax.rsqrt(jnp.mean(o * o, axis=0, keepdims=True) + EPS)
        o_ref[0, :, h * dh:(h + 1) * dh] = o.T * ga_ref[:, h * dh:(h + 1) * dh]


def _dsa(z, aux, ga, n_sel):
    B, T, _ = z.shape
    width = ga.shape[1]
    dh = width // A_HEADS
    assert n_sel % Q_BLOCK == 0 and T % (2 * Q_BLOCK) == 0 and dh == Z_AV - Z_AK
    kern = functools.partial(_dsa_kernel, n_sel=n_sel, dh=dh)
    return pl.pallas_call(
        kern,
        grid=(B, T // Q_BLOCK),
        in_specs=[pl.BlockSpec((1, Q_BLOCK, width), lambda b_, i: (b_, i, Z_AQ // width)),
                  pl.BlockSpec((1, Q_BLOCK, width), lambda b_, i: (b_, i, Z_IQ // width)),
                  pl.BlockSpec((1, T, LANES), lambda b_, i: (b_, 0, Z_AK // LANES)),
                  pl.BlockSpec((1, T, LANES), lambda b_, i: (b_, 0, Z_IK // LANES)),
                  pl.BlockSpec((1, AUX_ROWS, T), lambda b_, i: (b_, 0, 0)),
                  pl.BlockSpec((1, width), lambda b_, i: (0, 0))],
        out_specs=pl.BlockSpec((1, Q_BLOCK, width), lambda b_, i: (b_, i, 0)),
        out_shape=jax.ShapeDtypeStruct((B, T, width), F32),
        scratch_shapes=[pltpu.VMEM((T, Q_BLOCK), F32),
                        pltpu.VMEM((SUBLANES, Q_BLOCK), F32),
                        pltpu.VMEM((dh, A_HEADS * Q_BLOCK), F32),
                        pltpu.VMEM((2 * Q_BLOCK, 2 * Q_BLOCK), MXU_DT),
                        pltpu.VMEM((A_HEADS, 2 * Q_BLOCK, Q_BLOCK), F32)],
        compiler_params=_params(("parallel", "arbitrary")),
        name="dsa",
    )(z, z, z, z, aux, ga)


def _post_kernel(hm_ref, ha_ref, x_ref, wo_ref, g1_ref, g_ref, sc_ref, sh_ref, wq_ref, sk_ref,
                 x1_ref, h2t_ref, st_ref):
    wm = hm_ref.shape[2]
    y = _dot(hm_ref[0].astype(MXU_DT), wo_ref[0:wm, :]) + _dot(ha_ref[0].astype(MXU_DT), wo_ref[wm:, :])
    x1 = x_ref[0] + g1_ref[0] * y
    x1_ref[0] = x1
    h2 = _modulated_norm(x1, g_ref[...], sc_ref[0], sh_ref[0])
    h2t_ref[...] = h2.T.astype(h2t_ref.dtype)
    qry = _dot(h2.astype(MXU_DT), wq_ref[...]).astype(MXU_DT)
    dk = sk_ref.shape[2]
    for hp in range(sk_ref.shape[0]):
        st_ref[hp] = _nt_dot(sk_ref[hp], qry[:, hp * dk:(hp + 1) * dk])


def _post(hm, ha, x, wo, g1, g, sc, sh, wq, sk, tm):
    B, T, D = x.shape
    nt = T // tm
    wm = hm.shape[2]
    NS, NK, DK = sk.shape
    return pl.pallas_call(
        _post_kernel,
        grid=(B, nt),
        in_specs=[pl.BlockSpec((1, tm, wm), lambda b_, i: (b_, i, 0)),
                  pl.BlockSpec((1, tm, ha.shape[2]), lambda b_, i: (b_, i, 0)),
                  pl.BlockSpec((1, tm, D), lambda b_, i: (b_, i, 0)),
                  pl.BlockSpec(wo.shape, lambda b_, i: (0, 0)),
                  pl.BlockSpec((1, 1, D), lambda b_, i: (b_, 0, 0)),
                  pl.BlockSpec((1, D), lambda b_, i: (0, 0)),
                  pl.BlockSpec((1, 1, D), lambda b_, i: (b_, 0, 0)),
                  pl.BlockSpec((1, 1, D), lambda b_, i: (b_, 0, 0)),
                  pl.BlockSpec(wq.shape, lambda b_, i: (0, 0)),
                  pl.BlockSpec(sk.shape, lambda b_, i: (0, 0, 0))],
        out_specs=[pl.BlockSpec((1, tm, D), lambda b_, i: (b_, i, 0)),
                   pl.BlockSpec((D, tm), lambda b_, i: (0, b_ * nt + i)),
                   pl.BlockSpec((NS, NK, tm), lambda b_, i: (0, 0, b_ * nt + i))],
        out_shape=[jax.ShapeDtypeStruct((B, T, D), F32),
                   jax.ShapeDtypeStruct((D, B * T), MXU_DT),
                   jax.ShapeDtypeStruct((NS, NK, B * T), F32)],
        compiler_params=_params(("parallel", "parallel")),
        name="post",
    )(hm, ha, x, wo, g1, g, sc, sh, wq, sk)


def _top_sorted(cur, n):
    R = cur.shape[0]
    rows = lax.broadcasted_iota(I32, cur.shape, 0)
    vals = []
    for k in range(n):
        mx = jnp.max(cur, axis=0, keepdims=True)
        vals.append(mx)
        if k + 1 < n:
            first = jnp.min(jnp.where(cur == mx, rows, R), axis=0, keepdims=True)
            cur = jnp.where(rows == first, -jnp.inf, cur)
    return vals


def _select_kernel(st_ref, d1_ref, e1_ref, e2_ref, cand_s):
    n_top = P_TOPK + 1
    pairs = [(a, b) for a in range(n_top) for b in range(n_top) if (a + 1) * (b + 1) <= n_top]
    for h in range(P_HEADS):
        s1 = st_ref[2 * h]
        s2 = st_ref[2 * h + 1]
        v1 = _top_sorted(s1, n_top)
        v2 = _top_sorted(s2, n_top)
        cand_s[...] = jnp.full(cand_s.shape, -jnp.inf, F32)
        for r, (a, b) in enumerate(pairs):
            cand_s[r:r + 1, :] = v1[a] + v2[b]
        top = _top_sorted(cand_s[...], n_top)
        thr = 0.5 * (top[P_TOPK - 1] + top[P_TOPK])
        zsum = jnp.zeros_like(thr)
        for k in range(P_TOPK):
            zsum = zsum + jnp.exp(top[k] - top[0])
        d1_ref[h] = jnp.exp(thr - s1 - v2[0])
        e1_ref[h] = jnp.exp(s1 - v1[0]) / zsum
        e2_ref[h] = jnp.exp(s2 - v2[0])


def _select(st, tl):
    NS, NK, N = st.shape
    n_top = P_TOPK + 1
    n_pairs = sum(1 for a in range(n_top) for b in range(n_top) if (a + 1) * (b + 1) <= n_top)
    cand_rows = -(-n_pairs // SUBLANES) * SUBLANES
    out = jax.ShapeDtypeStruct((P_HEADS, NK, N), F32)
    spec = pl.BlockSpec((P_HEADS, NK, tl), lambda i: (0, 0, i))
    return pl.pallas_call(
        _select_kernel,
        grid=(N // tl,),
        in_specs=[pl.BlockSpec((NS, NK, tl), lambda i: (0, 0, i))],
        out_specs=[spec, spec, spec],
        out_shape=[out, out, out],
        scratch_shapes=[pltpu.VMEM((cand_rows, tl), F32)],
        compiler_params=_params(("parallel",)),
        name="select",
    )(st)


def _gelu(a):
    return 0.5 * a * (1.0 + lax.erf(a * (2.0 ** -0.5)))


def _expert_kernel(h2t_ref, u_ref, vt_ref, d1_ref, e1_ref, e2_ref, o_ref, acc_s, p_s, *, lane_tile):
    e = pl.program_id(1)
    te = u_ref.shape[0]
    tn = h2t_ref.shape[1]
    rows_per_step = te // N_KEYS

    @pl.when(e == 0)
    def _():
        acc_s[...] = jnp.zeros(acc_s.shape, F32)

    a_t = _dot(u_ref[...], h2t_ref[...])
    for ii in range(rows_per_step):
        i = e * rows_per_step + ii
        for l0 in range(0, tn, lane_tile):
            d_rows = [d1_ref[h, pl.ds(i, 1), l0:l0 + lane_tile] for h in range(P_HEADS)]
            e_rows = [e1_ref[h, pl.ds(i, 1), l0:l0 + lane_tile] for h in range(P_HEADS)]
            for c0 in range(0, lane_tile, LANES):
                g = jnp.zeros((N_KEYS, LANES), F32)
                for h in range(P_HEADS):
                    e2 = e2_ref[h, :, l0 + c0:l0 + c0 + LANES]
                    g = g + jnp.where(e2 > d_rows[h][:, c0:c0 + LANES], e2 * e_rows[h][:, c0:c0 + LANES], 0.0)
                a_blk = a_t[ii * N_KEYS:(ii + 1) * N_KEYS, l0 + c0:l0 + c0 + LANES]
                p_s[ii * N_KEYS:(ii + 1) * N_KEYS, l0 + c0:l0 + c0 + LANES] = (g * _gelu(a_blk)).astype(p_s.dtype)
    acc_s[...] += _dot(vt_ref[...], p_s[...])

    @pl.when(e == pl.num_programs(1) - 1)
    def _():
        o_ref[...] = acc_s[...]


def _expert(h2t, u, vt, d1, e1, e2, tn, te):
    D, N = h2t.shape
    NE = u.shape[0]
    NK = d1.shape[1]
    kern = functools.partial(_expert_kernel, lane_tile=min(tn, 2 * LANES))
    hspec = pl.BlockSpec((P_HEADS, NK, tn), lambda t, e: (0, 0, t))
    return pl.pallas_call(
        kern,
        grid=(N // tn, NE // te),
        in_specs=[pl.BlockSpec((D, tn), lambda t, e: (0, t)),
                  pl.BlockSpec((te, D), lambda t, e: (e, 0)),
                  pl.BlockSpec((D, te), lambda t, e: (0, e)),
                  hspec, hspec, hspec],
        out_specs=pl.BlockSpec((D, tn), lambda t, e: (0, t)),
        out_shape=jax.ShapeDtypeStruct((D, N), F32),
        scratch_shapes=[pltpu.VMEM((D, tn), F32),
                        pltpu.VMEM((te, tn), MXU_DT)],
        compiler_params=_params(("parallel", "arbitrary")),
        name="expert",
    )(h2t, u, vt, d1, e1, e2)


def _final_kernel(x1_ref, pt_ref, g2_ref, gf_ref, o_ref, *, normalize):
    x2 = x1_ref[0] + g2_ref[0] * pt_ref[...].T
    if normalize:
        ms = jnp.mean(x2 * x2, axis=-1, keepdims=True)
        x2 = x2 * lax.rsqrt(ms + EPS) * gf_ref[...]
    o_ref[0] = x2


def _final(x1, pt, g2, gf, tm, normalize):
    B, T, D = x1.shape
    nt = T // tm
    return pl.pallas_call(
        functools.partial(_final_kernel, normalize=normalize),
        grid=(B, nt),
        in_specs=[pl.BlockSpec((1, tm, D), lambda b_, i: (b_, i, 0)),
                  pl.BlockSpec((D, tm), lambda b_, i: (0, b_ * nt + i)),
                  pl.BlockSpec((1, 1, D), lambda b_, i: (b_, 0, 0)),
                  pl.BlockSpec((1, D), lambda b_, i: (0, 0))],
        out_specs=pl.BlockSpec((1, tm, D), lambda b_, i: (b_, i, 0)),
        out_shape=jax.ShapeDtypeStruct((B, T, D), F32),
        compiler_params=_params(("parallel", "parallel")),
        name="final",
    )(x1, pt, g2, gf)


def _pack_in_weights(w_in, b_in):
    D = w_in.shape[0]
    o_mi, o_aq, o_ak, o_av, o_iq, o_ik, o_iw, o_end = 2048, 2056, 2568, 2632, 2696, 3208, 3272, 3280
    order = [(0, o_mi), (o_aq, o_ak), (o_iq, o_ik), (o_ak, o_av), (o_av, o_iq), (o_ik, o_iw),
             (o_mi, o_aq), (o_iw, o_end)]
    n_used = sum(hi - lo for lo, hi in order)
    w = jnp.concatenate([w_in[:, lo:hi] for lo, hi in order] + [jnp.zeros((D, Z_COLS - n_used), F32)], axis=1)
    b = jnp.concatenate([b_in[lo:hi] for lo, hi in order] + [jnp.zeros((Z_COLS - n_used,), F32)])
    aux_order = [(o_av, o_iq), (o_mi, o_aq), (o_iw, o_end)]
    n_aux = sum(hi - lo for lo, hi in aux_order)
    wt = jnp.concatenate([w_in[:, lo:hi] for lo, hi in aux_order] + [jnp.zeros((D, AUX_ROWS - n_aux), F32)], axis=1).T
    bt = jnp.concatenate([b_in[lo:hi] for lo, hi in aux_order] + [jnp.zeros((AUX_ROWS - n_aux,), F32)])
    return w.astype(MXU_DT), b.reshape(1, Z_COLS), wt.astype(MXU_DT), bt.reshape(AUX_ROWS, 1)


def _layer(x, ada, norm1_g, w_in, b_in, conv_qk, out_norm_m, out_norm_a, w_out, norm2_g,
           w_peer_q, peer_sub_keys, peer_u, peer_v, *, tm, mrows, tl, tn, te):
    B, T, D = x.shape
    sh1, sc1, g1, sh2, sc2, g2 = [a.reshape(B, 1, D) for a in jnp.split(ada, 6, axis=-1)]
    w, b, wt, bt = _pack_in_weights(w_in, b_in)
    z, aux = _inproj(x, norm1_g.reshape(1, D), sc1, sh1, w, b, wt, bt, tm)
    hm = _mlstm(z, aux, conv_qk, out_norm_m.reshape(1, -1), mrows)
    ha = _dsa(z, aux, out_norm_a.reshape(1, -1), min(TOPK_MAX, T // 4))
    sk = peer_sub_keys.reshape(2 * P_HEADS, N_KEYS, -1).astype(MXU_DT)
    x1, h2t, st = _post(hm, ha, x, w_out.astype(MXU_DT), g1, norm2_g.reshape(1, D), sc2, sh2,
                        w_peer_q.astype(MXU_DT), sk, tm)
    d1, e1, e2 = _select(st, tl)
    pt = _expert(h2t, peer_u.astype(MXU_DT), peer_v.T.astype(MXU_DT), d1, e1, e2, tn, te)
    return x1, pt, g2


def kernel(x, c, w_ada, b_ada, norm1_g, w_in, b_in, conv_qk, out_norm_m, out_norm_a, w_out, norm2_g,
           w_peer_q, peer_sub_keys, peer_u, peer_v, final_g):
    B, T, D = x.shape
    depth = w_ada.shape[0]
    tm = min(512, T)
    for l in range(depth):
        ada = _ada(c, w_ada[l], b_ada[l])
        x1, pt, g2 = _layer(x, ada, norm1_g[l], w_in[l], b_in[l], conv_qk[l], out_norm_m[l], out_norm_a[l],
                            w_out[l], norm2_g[l], w_peer_q[l], peer_sub_keys[l], peer_u[l], peer_v[l],
                            tm=tm, mrows=min(256, T), tl=256, tn=1024, te=1024)
        x = _final(x1, pt, g2, final_g.reshape(1, D), tm, normalize=(l + 1 == depth))
    return x
```
